```python
import math
import jax, jax.numpy as jnp
from jax import lax
import numpy as np

D_MODEL = 2048
BATCH = 4
SEQ = 4096
DEPTH = 1

N_META = 16
BLOCK = 128
N_HEADS = 8
HEAD_DIM = 128
ATTN_WIDTH = N_HEADS * HEAD_DIM
CONV_CH = D_MODEL // 2
CONV_K = 3
N_EXPERTS = 32
TOP_K = 4
D_FF = D_MODEL
SWIGLU_LIMIT = 7.0
SWIGLU_ALPHA = 1.702
RMS_EPS = 1e-5
SPLIT_SIZES = (CONV_CH, CONV_CH, CONV_CH, ATTN_WIDTH, ATTN_WIDTH, ATTN_WIDTH, D_MODEL, D_MODEL)
IN_WIDTH = sum(SPLIT_SIZES)
SPLIT_POINTS = tuple(int(s) for s in np.cumsum(SPLIT_SIZES)[:-1])

kernel_name = "hybrid_conv_stickbreak_moe_block"


def rmsnorm(x, g):
    xf = x.astype(jnp.float32)
    y = xf * lax.rsqrt(jnp.mean(xf * xf, axis=-1, keepdims=True) + RMS_EPS)
    return (y * g.astype(jnp.float32)).astype(x.dtype)


def causal_depthwise_conv(u, w):
    l = u.shape[1]
    up = jnp.pad(u, ((0, 0), (CONV_K - 1, 0), (0, 0)))
    y = up[:, 0:l] * w[0]
    for j in range(1, CONV_K):
        y = y + up[:, j:j + l] * w[j]
    return y


def stick_breaking_attention(q, k, v):
    b, l, h, dh = q.shape
    pad = BLOCK - N_META

    def prep(t):
        return jnp.pad(t, ((0, 0), (pad, 0), (0, 0), (0, 0))).transpose(0, 2, 1, 3)

    qp, kp, vp = prep(q), prep(k), prep(v)
    lp = l + pad
    n_blocks = lp // BLOCK
    key_pos = jnp.arange(lp)
    key_valid = key_pos >= pad
    scale = 1.0 / math.sqrt(dh)

    def one_block(i):
        q_blk = lax.dynamic_slice_in_dim(qp, i * BLOCK, BLOCK, axis=2)
        z = jnp.einsum('bhqd,bhkd->bhqk', q_blk, kp).astype(jnp.float32) * scale
        q_pos = i * BLOCK + jnp.arange(BLOCK)
        mask = (key_pos[None, :] < q_pos[:, None]) & key_valid[None, :]
        log_beta = jax.nn.log_sigmoid(z)
        log_rest = jnp.where(mask, jax.nn.log_sigmoid(-z), 0.0)
        later = lax.cumsum(log_rest, axis=3, reverse=True) - log_rest
        weights = jnp.where(mask, jnp.exp(log_beta + later), 0.0)
        return jnp.einsum('bhqk,bhkd->bhqd', weights.astype(vp.dtype), vp)

    o = lax.map(one_block, jnp.arange(n_blocks))
    o = o.transpose(1, 0, 3, 2, 4).reshape(b, lp, h, dh)
    return o[:, pad:]


def clamped_swiglu(gate, up):
    gate = jnp.minimum(gate, SWIGLU_LIMIT)
    up = jnp.clip(up, -SWIGLU_LIMIT, SWIGLU_LIMIT)
    return (up + 1.0) * (gate * jax.nn.sigmoid(SWIGLU_ALPHA * gate))


def moe(h, w_router, b_router, w_gate, b_gate, w_up, b_up, w_down, b_down):
    bsz, l, d = h.shape
    n = h.reshape(bsz * l, d)
    logits = (n @ w_router + b_router).astype(jnp.float32)
    top_v, top_i = lax.top_k(logits, TOP_K)
    top_w = jax.nn.softmax(top_v, axis=-1)
    combine = jnp.einsum('nk,nke->ne', top_w, jax.nn.one_hot(top_i, N_EXPERTS, dtype=jnp.float32))
    combine = combine.astype(n.dtype)
    out = jnp.zeros_like(n)
    for e in range(N_EXPERTS):
        act = clamped_swiglu(n @ w_gate[e] + b_gate[e], n @ w_up[e] + b_up[e])
        out = out + combine[:, e:e + 1] * (act @ w_down[e] + b_down[e])
    return out.reshape(bsz, l, d)


def setup_inputs(seed: int = 0) -> dict:
    key = jax.random.key(seed)
    ks = jax.random.split(key, 20)
    f32 = jnp.float32
    nrm = lambda k, shape, s: jax.random.normal(k, shape, f32) * s
    D, L = D_MODEL, DEPTH
    return {
        "x": nrm(ks[0], (BATCH, SEQ, D), 1.0),
        "meta_tokens": nrm(ks[1], (N_META, D), 1.0),
        "g_mix": 1.0 + nrm(ks[2], (L, D), 0.02),
        "w_in": nrm(ks[3], (L, D, IN_WIDTH), D ** -0.5),
        "conv_w": nrm(ks[4], (L, CONV_K, CONV_CH), CONV_K ** -0.5),
        "w_conv_out": nrm(ks[5], (L, CONV_CH, D), CONV_CH ** -0.5),
        "w_attn_out": nrm(ks[6], (L, ATTN_WIDTH, D), ATTN_WIDTH ** -0.5),
        "w_o": nrm(ks[7], (L, D, D), D ** -0.5),
        "g_ffn": 1.0 + nrm(ks[8], (L, D), 0.02),
        "w_router": nrm(ks[9], (L, D, N_EXPERTS), D ** -0.5),
        "b_router": nrm(ks[10], (L, N_EXPERTS), 0.01),
        "w_gate": nrm(ks[11], (L, N_EXPERTS, D, D_FF), D ** -0.5),
        "b_gate": nrm(ks[12], (L, N_EXPERTS, D_FF), 0.01),
        "w_up": nrm(ks[13], (L, N_EXPERTS, D, D_FF), D ** -0.5),
        "b_up": nrm(ks[14], (L, N_EXPERTS, D_FF), 0.01),
        "w_down": nrm(ks[15], (L, N_EXPERTS, D_FF, D), D_FF ** -0.5),
        "b_down": nrm(ks[16], (L, N_EXPERTS, D), 0.01),
        "g_final": 1.0 + nrm(ks[17], (D,), 0.02),
    }


def reference(x, meta_tokens, g_mix, w_in, conv_w, w_conv_out, w_attn_out, w_o,
              g_ffn, w_router, b_router, w_gate, b_gate, w_up, b_up, w_down, b_down, g_final):
    bsz = x.shape[0]
    meta = jnp.broadcast_to(meta_tokens[None].astype(x.dtype), (bsz, N_META, D_MODEL))
    h_res = jnp.concatenate([meta, x], axis=1)
    l = h_res.shape[1]
    for layer in range(DEPTH):
        h = rmsnorm(h_res, g_mix[layer])
        proj = h @ w_in[layer]
        u, b_post, c_pre, q, k, v, gate_c, gate_a = jnp.split(proj, SPLIT_POINTS, axis=-1)
        y_conv = (b_post * causal_depthwise_conv(c_pre * u, conv_w[layer])) @ w_conv_out[layer]
        o = stick_breaking_attention(
            q.reshape(bsz, l, N_HEADS, HEAD_DIM),
            k.reshape(bsz, l, N_HEADS, HEAD_DIM),
            v.reshape(bsz, l, N_HEADS, HEAD_DIM))
        y_attn = o.reshape(bsz, l, ATTN_WIDTH) @ w_attn_out[layer]
        merged = jax.nn.sigmoid(gate_c) * y_conv + jax.nn.sigmoid(gate_a) * y_attn
        h_res = h_res + merged @ w_o[layer]
        h_res = h_res + moe(rmsnorm(h_res, g_ffn[layer]), w_router[layer], b_router[layer],
                            w_gate[layer], b_gate[layer], w_up[layer], b_up[layer],
                            w_down[layer], b_down[layer])
    return rmsnorm(h_res, g_final)[:, N_META:]
```

```python
import functools
import math

import jax
import jax.numpy as jnp
from jax import lax
from jax.experimental import pallas as pl
from jax.experimental.pallas import tpu as pltpu

F32 = jnp.float32
BF16 = jnp.bfloat16
U32 = jnp.uint32
I32 = jnp.int32

D_MODEL = 2048
N_META = 16
N_HEADS = 8
HEAD_DIM = 128
ATTN_WIDTH = N_HEADS * HEAD_DIM
CONV_CH = D_MODEL // 2
CONV_K = 3
N_EXPERTS = 32
TOP_K = 4
D_FF = D_MODEL
SWIGLU_LIMIT = 7.0
SWIGLU_ALPHA = 1.702
RMS_EPS = 1e-5
IN_WIDTH = 3 * CONV_CH + 3 * ATTN_WIDTH + 2 * D_MODEL

LANES = 128
SUBLANES = 8
META_ROWS = 128
LOG2E = 1.4426950408889634
VMEM_LIMIT = 56 * 1024 * 1024

RMS_TM = 512
PROJ_TM = 1024
PROJ_TN = 1024
ATT_TQ = 512
ATT_TK = 256
MIX_TM = 256
DISPATCH_TM = 256
EXPERT_TR = 1024
EXPERT_FC = 256
COMBINE_TM = 128
WORDS = D_MODEL // (2 * LANES)


def _cparams(sem, vmem=VMEM_LIMIT):
    return pltpu.CompilerParams(dimension_semantics=sem, vmem_limit_bytes=vmem)


def _rmsnorm_kernel(x_ref, g_ref, o_ref):
    x = x_ref[...].astype(F32)
    ms = jnp.mean(x * x, axis=-1, keepdims=True)
    o_ref[...] = (x * lax.rsqrt(ms + RMS_EPS) * g_ref[...]).astype(o_ref.dtype)


def _rmsnorm_bf16(x, g, tm):
    m, d = x.shape
    return pl.pallas_call(
        _rmsnorm_kernel,
        grid=(m // tm,),
        in_specs=[pl.BlockSpec((tm, d), lambda i: (i, 0)),
                  pl.BlockSpec((1, d), lambda i: (0, 0))],
        out_specs=pl.BlockSpec((tm, d), lambda i: (i, 0)),
        out_shape=jax.ShapeDtypeStruct((m, d), BF16),
        compiler_params=_cparams(("arbitrary",)),
        name="rmsnorm",
    )(x, g.reshape(1, d))


def _matmul_kernel(x_ref, w_ref, o_ref):
    o_ref[...] = jnp.dot(x_ref[...], w_ref[...], preferred_element_type=F32).astype(o_ref.dtype)


def _matmul_bf16(x, w, tm, tn):
    m, k = x.shape
    _, n = w.shape
    return pl.pallas_call(
        _matmul_kernel,
        grid=(n // tn, m // tm),
        in_specs=[pl.BlockSpec((tm, k), lambda j, i: (i, 0)),
                  pl.BlockSpec((k, tn), lambda j, i: (0, j))],
        out_specs=pl.BlockSpec((tm, tn), lambda j, i: (i, j)),
        out_shape=jax.ShapeDtypeStruct((m, n), BF16),
        compiler_params=_cparams(("arbitrary", "arbitrary")),
        name="in_proj",
    )(x, w)


def _attn_kernel(q_ref, k_ref, v_ref, km_ref, vm_ref, u_ref, um_ref, o_ref, acc_ref, r_ref,
                 *, seq, tq, tk):
    zscale = LOG2E / math.sqrt(HEAD_DIM)
    n_diag = tq // tk

    def q_body(qi, carry):
        q0 = pl.multiple_of(qi * tq, tq)
        q = q_ref[0, pl.ds(q0, tq), :]
        acc_ref[...] = jnp.zeros_like(acc_ref)
        r_ref[...] = jnp.zeros_like(r_ref)

        def tile(k, v, u, mask):
            z2 = lax.dot_general(q, k, (((1,), (1,)), ((), ())),
                                 preferred_element_type=F32) * zscale
            e = jnp.exp2(-jnp.abs(z2))
            s2 = jnp.maximum(z2, 0.0) + jnp.log2(1.0 + e)
            if mask is not None:
                s2 = jnp.where(mask, s2, 0.0)
            cum = jnp.dot(s2.astype(BF16), u, preferred_element_type=F32)
            w = jnp.exp2(z2 - cum - r_ref[...])
            if mask is not None:
                w = jnp.where(mask, w, 0.0)
            acc_ref[...] += jnp.dot(w.astype(BF16), v, preferred_element_type=F32)
            r_ref[...] += cum[:, 0:1]

        u = u_ref[...]
        for d in reversed(range(n_diag)):
            k0 = pl.multiple_of(q0 + d * tk, tk)
            row = lax.broadcasted_iota(I32, (tq, tk), 0)
            col = lax.broadcasted_iota(I32, (tq, tk), 1)
            tile(k_ref[0, pl.ds(k0, tk), :], v_ref[0, pl.ds(k0, tk), :], u, col + d * tk < row)

        def full(jj, c):
            k0 = pl.multiple_of((qi * n_diag - 1 - jj) * tk, tk)
            tile(k_ref[0, pl.ds(k0, tk), :], v_ref[0, pl.ds(k0, tk), :], u, None)
            return c

        lax.fori_loop(0, qi * n_diag, full, 0)

        colm = lax.broadcasted_iota(I32, (tq, META_ROWS), 1)
        tile(km_ref[...], vm_ref[...], um_ref[...], colm >= META_ROWS - N_META)
        o_ref[0, pl.ds(q0, tq), :] = acc_ref[...].astype(o_ref.dtype)
        return carry

    lax.fori_loop(0, seq // tq, q_body, 0)


def _tri_incl(n):
    r = lax.broadcasted_iota(I32, (n, n), 0)
    c = lax.broadcasted_iota(I32, (n, n), 1)
    return (r >= c).astype(BF16)


def _attention(proj, proj_meta, bsz, seq, tq, tk):
    proj3 = proj.reshape(bsz, seq, IN_WIDTH)
    qb, kb, vb = (3 * CONV_CH) // HEAD_DIM, (3 * CONV_CH + ATTN_WIDTH) // HEAD_DIM, (3 * CONV_CH + 2 * ATTN_WIDTH) // HEAD_DIM
    out = pl.pallas_call(
        functools.partial(_attn_kernel, seq=seq, tq=tq, tk=tk),
        grid=(bsz, N_HEADS),
        in_specs=[pl.BlockSpec((1, seq, HEAD_DIM), lambda b, h: (b, 0, qb + h)),
                  pl.BlockSpec((1, seq, HEAD_DIM), lambda b, h: (b, 0, kb + h)),
                  pl.BlockSpec((1, seq, HEAD_DIM), lambda b, h: (b, 0, vb + h)),
                  pl.BlockSpec((META_ROWS, HEAD_DIM), lambda b, h: (0, kb + h)),
                  pl.BlockSpec((META_ROWS, HEAD_DIM), lambda b, h: (0, vb + h)),
                  pl.BlockSpec((tk, tk), lambda b, h: (0, 0)),
                  pl.BlockSpec((META_ROWS, META_ROWS), lambda b, h: (0, 0))],
        out_specs=pl.BlockSpec((1, seq, HEAD_DIM), lambda b, h: (b, 0, h)),
        out_shape=jax.ShapeDtypeStruct((bsz, seq, ATTN_WIDTH), BF16),
        scratch_shapes=[pltpu.VMEM((tq, HEAD_DIM), F32), pltpu.VMEM((tq, 1), F32)],
        compiler_params=_cparams(("arbitrary", "arbitrary")),
        name="stickbreak_attn",
    )(proj3, proj3, proj3, proj_meta, proj_meta, _tri_incl(tk), _tri_incl(META_ROWS))
    return out.reshape(bsz * seq, ATTN_WIDTH)


def _pack_rows(vals_f32, out_ref):
    bits = pltpu.bitcast(vals_f32.astype(BF16).astype(F32), U32)
    for s in range(WORDS):
        lo = bits[:, (2 * s) * LANES:(2 * s + 1) * LANES] >> 16
        hi = bits[:, (2 * s + 1) * LANES:(2 * s + 2) * LANES] & jnp.uint32(0xFFFF0000)
        out_ref[:, s, :] = hi | lo


def _unpack_chunk(words_u32):
    lo = pltpu.bitcast(words_u32 << 16, F32)
    hi = pltpu.bitcast(words_u32 & jnp.uint32(0xFFFF0000), F32)
    return lo, hi


def _mixer_kernel(u_ref, bp_ref, cp_ref, gc_ref, ga_ref, o_ref, x_ref, um_ref, cm_ref,
                  convw_ref, wc_ref, wa_ref, wo_ref, gffn_ref, wrh_ref, wrl_ref, br_ref, ltri_ref,
                  h1_ref, npk_ref, meta_ref, cnt_ref,
                  cu_ref, carry_ref, *, tm, tiles_per_seq):
    i = pl.program_id(0)
    first = (i % tiles_per_seq) == 0

    @pl.when(i == 0)
    def _():
        carry_ref[...] = jnp.zeros_like(carry_ref)

    @pl.when(first)
    def _():
        cum = cm_ref[...].astype(F32) * um_ref[...].astype(F32)
        cu_ref[0:SUBLANES, :] = cum[SUBLANES:2 * SUBLANES, :]

    @pl.when(jnp.logical_not(first))
    def _():
        cu_ref[0:SUBLANES, :] = cu_ref[tm:tm + SUBLANES, :]

    cu = cp_ref[...].astype(F32) * u_ref[...].astype(F32)
    cu_ref[SUBLANES:tm + SUBLANES, :] = cu
    cw = convw_ref[...]
    conv = (cu_ref[SUBLANES - 2:tm + SUBLANES - 2, :] * cw[0:1, :]
            + cu_ref[SUBLANES - 1:tm + SUBLANES - 1, :] * cw[1:2, :]
            + cu * cw[2:3, :])
    y_conv = jnp.dot((bp_ref[...].astype(F32) * conv).astype(BF16), wc_ref[...],
                     preferred_element_type=F32)
    y_attn = jnp.dot(o_ref[...], wa_ref[...], preferred_element_type=F32)
    merged = (jax.nn.sigmoid(gc_ref[...].astype(F32)) * y_conv
              + jax.nn.sigmoid(ga_ref[...].astype(F32)) * y_attn)
    h1 = x_ref[...] + jnp.dot(merged.astype(BF16), wo_ref[...], preferred_element_type=F32)
    h1_ref[...] = h1

    ms = jnp.mean(h1 * h1, axis=-1, keepdims=True)
    n = h1 * lax.rsqrt(ms + RMS_EPS) * gffn_ref[...]
    _pack_rows(n, npk_ref)

    n_hi = n.astype(BF16)
    n_lo = (n - n_hi.astype(F32)).astype(BF16)
    logits = (jnp.dot(n_hi, wrh_ref[...], preferred_element_type=F32)
              + jnp.dot(n_lo, wrh_ref[...], preferred_element_type=F32)
              + jnp.dot(n_hi, wrl_ref[...], preferred_element_type=F32)) + br_ref[...]
    lane = lax.broadcasted_iota(I32, (tm, LANES), 1)
    lg = jnp.where(lane < N_EXPERTS, logits, -jnp.inf)

    sels, tops, idxs = [], [], []
    for _ in range(TOP_K):
        m = jnp.max(lg, axis=-1, keepdims=True)
        idx = jnp.min(jnp.where(lg == m, lane, LANES), axis=-1, keepdims=True)
        sel = lane == idx
        sels.append(sel)
        tops.append(m)
        idxs.append(idx)
        lg = jnp.where(sel, -jnp.inf, lg)
    exps = [jnp.exp(t - tops[0]) for t in tops]
    denom = exps[0] + exps[1] + exps[2] + exps[3]
    wts = [e / denom for e in exps]

    onehot = jnp.zeros((tm, LANES), F32)
    for sel in sels:
        onehot = onehot + sel.astype(F32)
    base = carry_ref[0:1, :] + jnp.dot(ltri_ref[...], onehot.astype(BF16), preferred_element_type=F32)
    meta = jnp.zeros((tm, LANES), I32)
    for k in range(TOP_K):
        rank = jnp.sum(jnp.where(sels[k], base, 0.0), axis=-1, keepdims=True)
        meta = jnp.where(lane == k, idxs[k], meta)
        meta = jnp.where(lane == TOP_K + k, rank.astype(I32), meta)
        meta = jnp.where(lane == 2 * TOP_K + k, pltpu.bitcast(wts[k], I32), meta)
    meta_ref[...] = meta
    carry_ref[0:1, :] = carry_ref[0:1, :] + jnp.sum(onehot, axis=0, keepdims=True)
    cnt_ref[...] = jnp.broadcast_to(carry_ref[0:1, :], cnt_ref.shape).astype(I32)


def _tri_strict_lower(n):
    r = lax.broadcasted_iota(I32, (n, n), 0)
    c = lax.broadcasted_iota(I32, (n, n), 1)
    return (c < r).astype(BF16)


def _const_spec(shape):
    return pl.BlockSpec(shape, lambda i: (0,) * len(shape))


def _mixer(proj, proj_meta, attn_o, x2d, conv_w, wc, wa, wo, g_ffn, wr_hi, wr_lo, b_r, seq, tm):
    n_tok = x2d.shape[0]
    tiles_per_seq = seq // tm
    meta_blk = META_ROWS // (2 * SUBLANES) - 1
    in_specs = [
        pl.BlockSpec((tm, CONV_CH), lambda i: (i, 0)),
        pl.BlockSpec((tm, CONV_CH), lambda i: (i, 1)),
        pl.BlockSpec((tm, CONV_CH), lambda i: (i, 2)),
        pl.BlockSpec((tm, D_MODEL), lambda i: (i, 3)),
        pl.BlockSpec((tm, D_MODEL), lambda i: (i, 4)),
        pl.BlockSpec((tm, ATTN_WIDTH), lambda i: (i, 0)),
        pl.BlockSpec((tm, D_MODEL), lambda i: (i, 0)),
        pl.BlockSpec((2 * SUBLANES, CONV_CH), lambda i: (meta_blk, 0)),
        pl.BlockSpec((2 * SUBLANES, CONV_CH), lambda i: (meta_blk, 2)),
        _const_spec((SUBLANES, CONV_CH)),
        _const_spec((CONV_CH, D_MODEL)),
        _const_spec((ATTN_WIDTH, D_MODEL)),
        _const_spec((D_MODEL, D_MODEL)),
        _const_spec((1, D_MODEL)),
        _const_spec((D_MODEL, LANES)),
        _const_spec((D_MODEL, LANES)),
        _const_spec((1, LANES)),
        _const_spec((tm, tm)),
    ]
    out_specs = [
        pl.BlockSpec((tm, D_MODEL), lambda i: (i, 0)),
        pl.BlockSpec((tm, WORDS, LANES), lambda i: (i, 0, 0)),
        pl.BlockSpec((tm, LANES), lambda i: (i, 0)),
        _const_spec((SUBLANES, LANES)),
    ]
    out_shape = [
        jax.ShapeDtypeStruct((n_tok, D_MODEL), F32),
        jax.ShapeDtypeStruct((n_tok, WORDS, LANES), U32),
        jax.ShapeDtypeStruct((n_tok, LANES), I32),
        jax.ShapeDtypeStruct((SUBLANES, LANES), I32),
    ]
    conv_w8 = jnp.pad(conv_w, ((0, SUBLANES - CONV_K), (0, 0)))
    return pl.pallas_call(
        functools.partial(_mixer_kernel, tm=tm, tiles_per_seq=tiles_per_seq),
        grid=(n_tok // tm,),
        in_specs=in_specs,
        out_specs=out_specs,
        out_shape=out_shape,
        scratch_shapes=[pltpu.VMEM((tm + SUBLANES, CONV_CH), F32), pltpu.VMEM((SUBLANES, LANES), F32)],
        compiler_params=_cparams(("arbitrary",)),
        name="mixer_out",
    )(proj, proj, proj, proj, proj, attn_o, x2d, proj_meta, proj_meta, conv_w8, wc, wa, wo,
      g_ffn.reshape(1, D_MODEL), wr_hi, wr_lo, b_r, _tri_strict_lower(tm))


def _dispatch_kernel(eid_ref, rank_ref, offs_ref, npk_ref, xg_in_ref, xg_ref, sem, *, tm):
    del xg_in_ref
    t0 = pl.program_id(0) * tm

    def issue(t, c):
        for k in range(TOP_K):
            j = t * TOP_K + k
            dst = offs_ref[eid_ref[j]] + rank_ref[j]
            pltpu.make_async_copy(npk_ref.at[t0 + t], xg_ref.at[dst], sem).start()
        return c

    lax.fori_loop(0, tm, issue, 0)
    pltpu.make_async_copy(npk_ref.at[pl.ds(0, tm * TOP_K)], xg_ref.at[pl.ds(0, tm * TOP_K)], sem).wait()


def _dispatch(eid_flat, rank_flat, offs, npk, n_rows, tm):
    n_tok = npk.shape[0]
    xg0 = jnp.zeros((n_rows, WORDS, LANES), U32)
    smem_blk = pl.BlockSpec((tm * TOP_K,), lambda i: (i,), memory_space=pltpu.SMEM)
    return pl.pallas_call(
        functools.partial(_dispatch_kernel, tm=tm),
        grid=(n_tok // tm,),
        in_specs=[smem_blk, smem_blk,
                  pl.BlockSpec(memory_space=pltpu.SMEM),
                  pl.BlockSpec(memory_space=pl.ANY),
                  pl.BlockSpec(memory_space=pl.ANY)],
        out_specs=pl.BlockSpec(memory_space=pl.ANY),
        out_shape=jax.ShapeDtypeStruct((n_rows, WORDS, LANES), U32),
        scratch_shapes=[pltpu.SemaphoreType.DMA(())],
        input_output_aliases={4: 0},
        compiler_params=_cparams(("arbitrary",)),
        name="moe_dispatch",
    )(eid_flat, rank_flat, offs, npk, xg0)


def _expert_kernel(te_ref, tb_ref, nu_ref, x_ref, wg_ref, bg_ref, wu_ref, bu_ref, wd_ref, bd_ref,
                   o_ref, xs_ref, acc_ref, *, n_fc):
    del te_ref, tb_ref
    t = pl.program_id(0)
    f = pl.program_id(1)

    @pl.when(t < nu_ref[0])
    def _():
        @pl.when(f == 0)
        def _():
            for s in range(WORDS):
                lo, hi = _unpack_chunk(x_ref[:, s, :])
                xs_ref[:, (2 * s) * LANES:(2 * s + 1) * LANES] = lo.astype(BF16)
                xs_ref[:, (2 * s + 1) * LANES:(2 * s + 2) * LANES] = hi.astype(BF16)
            acc_ref[...] = jnp.zeros_like(acc_ref)

        x = xs_ref[...]
        gate = jnp.dot(x, wg_ref[0].astype(BF16), preferred_element_type=F32) + bg_ref[0]
        up = jnp.dot(x, wu_ref[0].astype(BF16), preferred_element_type=F32) + bu_ref[0]
        gate = jnp.minimum(gate, SWIGLU_LIMIT)
        up = jnp.clip(up, -SWIGLU_LIMIT, SWIGLU_LIMIT)
        act = (up + 1.0) * (gate * jax.nn.sigmoid(SWIGLU_ALPHA * gate))
        acc_ref[...] += jnp.dot(act.astype(BF16), wd_ref[0].astype(BF16), preferred_element_type=F32)

        @pl.when(f == n_fc - 1)
        def _():
            _pack_rows(acc_ref[...] + bd_ref[0], o_ref)


def _experts(tile_expert, tile_block, n_used, xg, w_gate, b_gate, w_up, b_up, w_down, b_down, tr, fc):
    n_rows = xg.shape[0]
    n_exp, d, dff = w_gate.shape
    n_fc = dff // fc
    grid_spec = pltpu.PrefetchScalarGridSpec(
        num_scalar_prefetch=3,
        grid=(n_rows // tr, n_fc),
        in_specs=[
            pl.BlockSpec((tr, WORDS, LANES), lambda t, f, te, tb, nu: (tb[t], 0, 0)),
            pl.BlockSpec((1, d, fc), lambda t, f, te, tb, nu: (te[t], 0, f)),
            pl.BlockSpec((1, 1, fc), lambda t, f, te, tb, nu: (te[t], 0, f)),
            pl.BlockSpec((1, d, fc), lambda t, f, te, tb, nu: (te[t], 0, f)),
            pl.BlockSpec((1, 1, fc), lambda t, f, te, tb, nu: (te[t], 0, f)),
            pl.BlockSpec((1, fc, d), lambda t, f, te, tb, nu: (te[t], f, 0)),
            pl.BlockSpec((1, 1, d), lambda t, f, te, tb, nu: (te[t], 0, 0)),
        ],
        out_specs=pl.BlockSpec((tr, WORDS, LANES), lambda t, f, te, tb, nu: (tb[t], 0, 0)),
        scratch_shapes=[pltpu.VMEM((tr, d), BF16), pltpu.VMEM((tr, d), F32)],
    )
    return pl.pallas_call(
        functools.partial(_expert_kernel, n_fc=n_fc),
        grid_spec=grid_spec,
        out_shape=jax.ShapeDtypeStruct((n_rows, WORDS, LANES), U32),
        input_output_aliases={3: 0},
        compiler_params=_cparams(("arbitrary", "arbitrary")),
        name="moe_experts",
    )(tile_expert, tile_block, n_used, xg, w_gate, b_gate.reshape(n_exp, 1, dff), w_up,
      b_up.reshape(n_exp, 1, dff), w_down, b_down.reshape(n_exp, 1, d))


def _combine_kernel(eid_ref, rank_ref, offs_ref, og_ref, meta_ref, h1_ref, gfin_ref, out_ref,
                    gbuf_ref, h2_ref, sem, *, tm):
    def issue(t, c):
        for k in range(TOP_K):
            j = t * TOP_K + k
            src = offs_ref[eid_ref[j]] + rank_ref[j]
            pltpu.make_async_copy(og_ref.at[src], gbuf_ref.at[k * tm + t], sem).start()
        return c

    lax.fori_loop(0, tm, issue, 0)
    pltpu.make_async_copy(og_ref.at[pl.ds(0, tm * TOP_K)], gbuf_ref, sem).wait()

    wts = pltpu.bitcast(meta_ref[...], F32)
    wk = [wts[:, 2 * TOP_K + k:2 * TOP_K + k + 1] for k in range(TOP_K)]
    for s in range(WORDS):
        y_lo = jnp.zeros((tm, LANES), F32)
        y_hi = jnp.zeros((tm, LANES), F32)
        for k in range(TOP_K):
            lo, hi = _unpack_chunk(gbuf_ref[k * tm:(k + 1) * tm, s, :])
            y_lo = y_lo + wk[k] * lo
            y_hi = y_hi + wk[k] * hi
        c0, c1 = (2 * s) * LANES, (2 * s + 1) * LANES
        h2_ref[:, c0:c0 + LANES] = h1_ref[:, c0:c0 + LANES] + y_lo
        h2_ref[:, c1:c1 + LANES] = h1_ref[:, c1:c1 + LANES] + y_hi
    h2 = h2_ref[...]
    ms = jnp.mean(h2 * h2, axis=-1, keepdims=True)
    out_ref[...] = h2 * lax.rsqrt(ms + RMS_EPS) * gfin_ref[...]


def _combine(eid_flat, rank_flat, offs, og, meta, h1, g_final, tm):
    n_tok = h1.shape[0]
    smem_blk = pl.BlockSpec((tm * TOP_K,), lambda i: (i,), memory_space=pltpu.SMEM)
    return pl.pallas_call(
        functools.partial(_combine_kernel, tm=tm),
        grid=(n_tok // tm,),
        in_specs=[smem_blk, smem_blk,
                  pl.BlockSpec(memory_space=pltpu.SMEM),
                  pl.BlockSpec(memory_space=pl.ANY),
                  pl.BlockSpec((tm, LANES), lambda i: (i, 0)),
                  pl.BlockSpec((tm, D_MODEL), lambda i: (i, 0)),
                  pl.BlockSpec((1, D_MODEL), lambda i: (0, 0))],
        out_specs=pl.BlockSpec((tm, D_MODEL), lambda i: (i, 0)),
        out_shape=jax.ShapeDtypeStruct((n_tok, D_MODEL), F32),
        scratch_shapes=[pltpu.VMEM((tm * TOP_K, WORDS, LANES), U32),
                        pltpu.VMEM((tm, D_MODEL), F32),
                        pltpu.SemaphoreType.DMA(())],
        compiler_params=_cparams(("arbitrary",)),
        name="moe_combine",
    )(eid_flat, rank_flat, offs, og, meta, h1, g_final.reshape(1, D_MODEL))


def _routing_tables(counts, tr, n_tiles):
    ntile = (counts + tr - 1) // tr
    tiles_cum = jnp.cumsum(ntile)
    offs = ((tiles_cum - ntile) * tr).astype(I32)
    n_used = tiles_cum[-1]
    t = jnp.minimum(jnp.arange(n_tiles, dtype=I32), n_used - 1)
    tile_expert = jnp.sum((tiles_cum[None, :] <= t[:, None]).astype(I32), axis=1)
    return offs, tile_expert, t.astype(I32), n_used.reshape(1).astype(I32)


def kernel(x, meta_tokens, g_mix, w_in, conv_w, w_conv_out, w_attn_out, w_o, g_ffn, w_router,
           b_router, w_gate, b_gate, w_up, b_up, w_down, b_down, g_final):
    assert g_mix.shape[0] == 1, "single-layer trunk"
    bsz, seq, d = x.shape
    n_tok = bsz * seq
    x2d = x.reshape(n_tok, d)

    w_in_bf = w_in[0].astype(BF16)
    hn = _rmsnorm_bf16(x2d, g_mix[0], RMS_TM)
    proj = _matmul_bf16(hn, w_in_bf, PROJ_TM, PROJ_TN)
    meta_pad = jnp.pad(meta_tokens.astype(x.dtype), ((META_ROWS - N_META, 0), (0, 0)))
    hn_meta = _rmsnorm_bf16(meta_pad, g_mix[0], META_ROWS)
    proj_meta = _matmul_bf16(hn_meta, w_in_bf, META_ROWS, PROJ_TN)

    attn_o = _attention(proj, proj_meta, bsz, seq, ATT_TQ, ATT_TK)

    wr = jnp.pad(w_router[0], ((0, 0), (0, LANES - N_EXPERTS)))
    wr_hi = wr.astype(BF16)
    wr_lo = (wr - wr_hi.astype(F32)).astype(BF16)
    b_r = jnp.pad(b_router[0], (0, LANES - N_EXPERTS)).reshape(1, LANES)
    h1, npk, meta, cnt = _mixer(proj, proj_meta, attn_o, x2d, conv_w[0],
                                w_conv_out[0].astype(BF16), w_attn_out[0].astype(BF16),
                                w_o[0].astype(BF16), g_ffn[0], wr_hi, wr_lo, b_r, seq, MIX_TM)

    n_tiles = (n_tok * TOP_K) // EXPERT_TR + N_EXPERTS
    offs, tile_expert, tile_block, n_used = _routing_tables(cnt[0, :N_EXPERTS], EXPERT_TR, n_tiles)
    eid_flat = meta[:, 0:TOP_K].reshape(-1)
    rank_flat = meta[:, TOP_K:2 * TOP_K].reshape(-1)

    xg = _dispatch(eid_flat, rank_flat, offs, npk, n_tiles * EXPERT_TR, DISPATCH_TM)
    og = _experts(tile_expert, tile_block, n_used, xg, w_gate[0], b_gate[0], w_up[0], b_up[0],
                  w_down[0], b_down[0], EXPERT_TR, EXPERT_FC)
    out = _combine(eid_flat, rank_flat, offs, og, meta, h1, g_final, COMBINE_TM)
    return out.reshape(bsz, seq, d)
```

```python
import functools
import math

import jax
import jax.numpy as jnp
from jax import lax
from jax.experimental import pallas as pl
from jax.experimental.pallas import tpu as pltpu

F32 = jnp.float32
BF16 = jnp.bfloat16
I32 = jnp.int32

D_MODEL = 2048
N_META = 16
N_HEADS = 8
HEAD_DIM = 128
ATTN_WIDTH = N_HEADS * HEAD_DIM
CONV_CH = D_MODEL // 2
CONV_K = 3
N_EXPERTS = 32
TOP_K = 4
D_FF = D_MODEL
SWIGLU_LIMIT = 7.0
SWIGLU_ALPHA = 1.702
RMS_EPS = 1e-5
IN_WIDTH = 3 * CONV_CH + 3 * ATTN_WIDTH + 2 * D_MODEL

LANES = 128
SUBLANES = 8
META_ROWS = 128
LOG2E = 1.4426950408889634
VMEM_LIMIT = 56 * 1024 * 1024

RMS_TM = 512
PROJ_TM = 1024
PROJ_TN = 1024
ATT_TQ = 512
ATT_TK = 256
MIX_TM = 256
DISPATCH_TM = 256
EXPERT_TR = 768
EXPERT_FC = 256
COMBINE_TM = 128
TOK_ROWS = SUBLANES
TOK_HALVES = D_MODEL // (TOK_ROWS * LANES)


def _tok_shape(n_rows):
    return (TOK_HALVES, n_rows * TOK_ROWS, LANES)


def _cparams(sem, vmem=VMEM_LIMIT):
    return pltpu.CompilerParams(dimension_semantics=sem, vmem_limit_bytes=vmem)


def _rmsnorm_kernel(x_ref, g_ref, o_ref):
    x = x_ref[...].astype(F32)
    ms = jnp.mean(x * x, axis=-1, keepdims=True)
    o_ref[...] = (x * lax.rsqrt(ms + RMS_EPS) * g_ref[...]).astype(o_ref.dtype)


def _rmsnorm_bf16(x, g, tm):
    m, d = x.shape
    return pl.pallas_call(
        _rmsnorm_kernel,
        grid=(m // tm,),
        in_specs=[pl.BlockSpec((tm, d), lambda i: (i, 0)),
                  pl.BlockSpec((1, d), lambda i: (0, 0))],
        out_specs=pl.BlockSpec((tm, d), lambda i: (i, 0)),
        out_shape=jax.ShapeDtypeStruct((m, d), BF16),
        compiler_params=_cparams(("arbitrary",)),
        name="rmsnorm",
    )(x, g.reshape(1, d))


def _matmul_kernel(x_ref, w_ref, o_ref):
    o_ref[...] = jnp.dot(x_ref[...], w_ref[...], preferred_element_type=F32).astype(o_ref.dtype)


def _matmul_bf16(x, w, tm, tn):
    m, k = x.shape
    _, n = w.shape
    return pl.pallas_call(
        _matmul_kernel,
        grid=(n // tn, m // tm),
        in_specs=[pl.BlockSpec((tm, k), lambda j, i: (i, 0)),
                  pl.BlockSpec((k, tn), lambda j, i: (0, j))],
        out_specs=pl.BlockSpec((tm, tn), lambda j, i: (i, j)),
        out_shape=jax.ShapeDtypeStruct((m, n), BF16),
        compiler_params=_cparams(("arbitrary", "arbitrary")),
        name="in_proj",
    )(x, w)


def _attn_kernel(q_ref, k_ref, v_ref, km_ref, vm_ref, u_ref, um_ref, o_ref, acc_ref, r_ref,
                 *, seq, tq, tk):
    zscale = LOG2E / math.sqrt(HEAD_DIM)
    n_diag = tq // tk

    def q_body(qi, carry):
        q0 = pl.multiple_of(qi * tq, tq)
        q = q_ref[0, pl.ds(q0, tq), :]
        acc_ref[...] = jnp.zeros_like(acc_ref)
        r_ref[...] = jnp.zeros_like(r_ref)

        def tile(k, v, u, mask):
            z2 = lax.dot_general(q, k, (((1,), (1,)), ((), ())),
                                 preferred_element_type=F32) * zscale
            e = jnp.exp2(-jnp.abs(z2))
            s2 = jnp.maximum(z2, 0.0) + jnp.log2(1.0 + e)
            if mask is not None:
                s2 = jnp.where(mask, s2, 0.0)
            cum = jnp.dot(s2.astype(BF16), u, preferred_element_type=F32)
            w = jnp.exp2(z2 - cum - r_ref[...])
            if mask is not None:
                w = jnp.where(mask, w, 0.0)
            acc_ref[...] += jnp.dot(w.astype(BF16), v, preferred_element_type=F32)
            r_ref[...] += cum[:, 0:1]

        u = u_ref[...]
        for d in reversed(range(n_diag)):
            k0 = pl.multiple_of(q0 + d * tk, tk)
            row = lax.broadcasted_iota(I32, (tq, tk), 0)
            col = lax.broadcasted_iota(I32, (tq, tk), 1)
            tile(k_ref[0, pl.ds(k0, tk), :], v_ref[0, pl.ds(k0, tk), :], u, col + d * tk < row)

        def full(jj, c):
            for d in reversed(range(n_diag)):
                k0 = pl.multiple_of(((qi - 1 - jj) * n_diag + d) * tk, tk)
                tile(k_ref[0, pl.ds(k0, tk), :], v_ref[0, pl.ds(k0, tk), :], u, None)
            return c

        lax.fori_loop(0, qi, full, 0)

        colm = lax.broadcasted_iota(I32, (tq, META_ROWS), 1)
        tile(km_ref[...], vm_ref[...], um_ref[...], colm >= META_ROWS - N_META)
        o_ref[0, pl.ds(q0, tq), :] = acc_ref[...].astype(o_ref.dtype)
        return carry

    lax.fori_loop(0, seq // tq, q_body, 0)


def _tri_incl(n):
    r = lax.broadcasted_iota(I32, (n, n), 0)
    c = lax.broadcasted_iota(I32, (n, n), 1)
    return (r >= c).astype(BF16)


def _attention(proj, proj_meta, bsz, seq, tq, tk):
    proj3 = proj.reshape(bsz, seq, IN_WIDTH)
    qb, kb, vb = (3 * CONV_CH) // HEAD_DIM, (3 * CONV_CH + ATTN_WIDTH) // HEAD_DIM, (3 * CONV_CH + 2 * ATTN_WIDTH) // HEAD_DIM
    out = pl.pallas_call(
        functools.partial(_attn_kernel, seq=seq, tq=tq, tk=tk),
        grid=(bsz, N_HEADS),
        in_specs=[pl.BlockSpec((1, seq, HEAD_DIM), lambda b, h: (b, 0, qb + h)),
                  pl.BlockSpec((1, seq, HEAD_DIM), lambda b, h: (b, 0, kb + h)),
                  pl.BlockSpec((1, seq, HEAD_DIM), lambda b, h: (b, 0, vb + h)),
                  pl.BlockSpec((META_ROWS, HEAD_DIM), lambda b, h: (0, kb + h)),
                  pl.BlockSpec((META_ROWS, HEAD_DIM), lambda b, h: (0, vb + h)),
                  pl.BlockSpec((tk, tk), lambda b, h: (0, 0)),
                  pl.BlockSpec((META_ROWS, META_ROWS), lambda b, h: (0, 0))],
        out_specs=pl.BlockSpec((1, seq, HEAD_DIM), lambda b, h: (b, 0, h)),
        out_shape=jax.ShapeDtypeStruct((bsz, seq, ATTN_WIDTH), BF16),
        scratch_shapes=[pltpu.VMEM((tq, HEAD_DIM), F32), pltpu.VMEM((tq, 1), F32)],
        compiler_params=_cparams(("arbitrary", "arbitrary")),
        name="stickbreak_attn",
    )(proj3, proj3, proj3, proj_meta, proj_meta, _tri_incl(tk), _tri_incl(META_ROWS))
    return out.reshape(bsz * seq, ATTN_WIDTH)


def _slab_idx(first_tok, n_tok, c):
    h, s = divmod(c, TOK_ROWS)
    return (h, pl.ds(first_tok * TOK_ROWS + s, n_tok, stride=TOK_ROWS), slice(None))


N_CHUNKS = D_MODEL // LANES


def _store_slabs(vals, out_ref, n_tok):
    for c in range(N_CHUNKS):
        out_ref[_slab_idx(0, n_tok, c)] = vals[:, c * LANES:(c + 1) * LANES]


def _mixer_kernel(u_ref, bp_ref, cp_ref, gc_ref, ga_ref, o_ref, x_ref, um_ref, cm_ref,
                  convw_ref, wc_ref, wa_ref, wo_ref, gffn_ref, wrh_ref, wrl_ref, br_ref, ltri_ref,
                  h1_ref, npk_ref, meta_ref, cnt_ref,
                  cu_ref, carry_ref, *, tm, tiles_per_seq):
    i = pl.program_id(0)
    first = (i % tiles_per_seq) == 0

    @pl.when(i == 0)
    def _():
        carry_ref[...] = jnp.zeros_like(carry_ref)

    @pl.when(first)
    def _():
        cum = cm_ref[...].astype(F32) * um_ref[...].astype(F32)
        cu_ref[0:SUBLANES, :] = cum[SUBLANES:2 * SUBLANES, :]

    @pl.when(jnp.logical_not(first))
    def _():
        cu_ref[0:SUBLANES, :] = cu_ref[tm:tm + SUBLANES, :]

    cu = cp_ref[...].astype(F32) * u_ref[...].astype(F32)
    cu_ref[SUBLANES:tm + SUBLANES, :] = cu
    cw = convw_ref[...]
    conv = (cu_ref[SUBLANES - 2:tm + SUBLANES - 2, :] * cw[0:1, :]
            + cu_ref[SUBLANES - 1:tm + SUBLANES - 1, :] * cw[1:2, :]
            + cu * cw[2:3, :])
    y_conv = jnp.dot((bp_ref[...].astype(F32) * conv).astype(BF16), wc_ref[...],
                     preferred_element_type=F32)
    y_attn = jnp.dot(o_ref[...], wa_ref[...], preferred_element_type=F32)
    merged = (jax.nn.sigmoid(gc_ref[...].astype(F32)) * y_conv
              + jax.nn.sigmoid(ga_ref[...].astype(F32)) * y_attn)
    h1 = x_ref[...] + jnp.dot(merged.astype(BF16), wo_ref[...], preferred_element_type=F32)
    h1_ref[...] = h1

    ms = jnp.mean(h1 * h1, axis=-1, keepdims=True)
    n = h1 * lax.rsqrt(ms + RMS_EPS) * gffn_ref[...]
    _store_slabs(n, npk_ref, tm)

    n_hi = n.astype(BF16)
    n_lo = (n - n_hi.astype(F32)).astype(BF16)
    logits = (jnp.dot(n_hi, wrh_ref[...], preferred_element_type=F32)
              + jnp.dot(n_lo, wrh_ref[...], preferred_element_type=F32)
              + jnp.dot(n_hi, wrl_ref[...], preferred_element_type=F32)) + br_ref[...]
    lane = lax.broadcasted_iota(I32, (tm, LANES), 1)
    lg = jnp.where(lane < N_EXPERTS, logits, -jnp.inf)

    sels, tops, idxs = [], [], []
    for _ in range(TOP_K):
        m = jnp.max(lg, axis=-1, keepdims=True)
        idx = jnp.min(jnp.where(lg == m, lane, LANES), axis=-1, keepdims=True)
        sel = lane == idx
        sels.append(sel)
        tops.append(m)
        idxs.append(idx)
        lg = jnp.where(sel, -jnp.inf, lg)
    exps = [jnp.exp(t - tops[0]) for t in tops]
    denom = exps[0] + exps[1] + exps[2] + exps[3]
    wts = [e / denom for e in exps]

    onehot = jnp.zeros((tm, LANES), F32)
    for sel in sels:
        onehot = onehot + sel.astype(F32)
    base = carry_ref[0:1, :] + jnp.dot(ltri_ref[...], onehot.astype(BF16), preferred_element_type=F32)
    meta = jnp.zeros((tm, LANES), I32)
    for k in range(TOP_K):
        rank = jnp.sum(jnp.where(sels[k], base, 0.0), axis=-1, keepdims=True)
        meta = jnp.where(lane == k, idxs[k], meta)
        meta = jnp.where(lane == TOP_K + k, rank.astype(I32), meta)
        meta = jnp.where(lane == 2 * TOP_K + k, pltpu.bitcast(wts[k], I32), meta)
    meta_ref[...] = meta
    carry_ref[0:1, :] = carry_ref[0:1, :] + jnp.sum(onehot, axis=0, keepdims=True)
    cnt_ref[...] = jnp.broadcast_to(carry_ref[0:1, :], cnt_ref.shape).astype(I32)


def _tri_strict_lower(n):
    r = lax.broadcasted_iota(I32, (n, n), 0)
    c = lax.broadcasted_iota(I32, (n, n), 1)
    return (c < r).astype(BF16)


def _const_spec(shape):
    return pl.BlockSpec(shape, lambda i: (0,) * len(shape))


def _mixer(proj, proj_meta, attn_o, x2d, conv_w, wc, wa, wo, g_ffn, wr_hi, wr_lo, b_r, seq, tm):
    n_tok = x2d.shape[0]
    tiles_per_seq = seq // tm
    meta_blk = META_ROWS // (2 * SUBLANES) - 1
    in_specs = [
        pl.BlockSpec((tm, CONV_CH), lambda i: (i, 0)),
        pl.BlockSpec((tm, CONV_CH), lambda i: (i, 1)),
        pl.BlockSpec((tm, CONV_CH), lambda i: (i, 2)),
        pl.BlockSpec((tm, D_MODEL), lambda i: (i, 3)),
        pl.BlockSpec((tm, D_MODEL), lambda i: (i, 4)),
        pl.BlockSpec((tm, ATTN_WIDTH), lambda i: (i, 0)),
        pl.BlockSpec((tm, D_MODEL), lambda i: (i, 0)),
        pl.BlockSpec((2 * SUBLANES, CONV_CH), lambda i: (meta_blk, 0)),
        pl.BlockSpec((2 * SUBLANES, CONV_CH), lambda i: (meta_blk, 2)),
        _const_spec((SUBLANES, CONV_CH)),
        _const_spec((CONV_CH, D_MODEL)),
        _const_spec((ATTN_WIDTH, D_MODEL)),
        _const_spec((D_MODEL, D_MODEL)),
        _const_spec((1, D_MODEL)),
        _const_spec((D_MODEL, LANES)),
        _const_spec((D_MODEL, LANES)),
        _const_spec((1, LANES)),
        _const_spec((tm, tm)),
    ]
    out_specs = [
        pl.BlockSpec((tm, D_MODEL), lambda i: (i, 0)),
        pl.BlockSpec(_tok_shape(tm), lambda i: (0, i, 0)),
        pl.BlockSpec((tm, LANES), lambda i: (i, 0)),
        _const_spec((SUBLANES, LANES)),
    ]
    out_shape = [
        jax.ShapeDtypeStruct((n_tok, D_MODEL), F32),
        jax.ShapeDtypeStruct(_tok_shape(n_tok), F32),
        jax.ShapeDtypeStruct((n_tok, LANES), I32),
        jax.ShapeDtypeStruct((SUBLANES, LANES), I32),
    ]
    conv_w8 = jnp.pad(conv_w, ((0, SUBLANES - CONV_K), (0, 0)))
    return pl.pallas_call(
        functools.partial(_mixer_kernel, tm=tm, tiles_per_seq=tiles_per_seq),
        grid=(n_tok // tm,),
        in_specs=in_specs,
        out_specs=out_specs,
        out_shape=out_shape,
        scratch_shapes=[pltpu.VMEM((tm + SUBLANES, CONV_CH), F32), pltpu.VMEM((SUBLANES, LANES), F32)],
        compiler_params=_cparams(("arbitrary",)),
        name="mixer_out",
    )(proj, proj, proj, proj, proj, attn_o, x2d, proj_meta, proj_meta, conv_w8, wc, wa, wo,
      g_ffn.reshape(1, D_MODEL), wr_hi, wr_lo, b_r, _tri_strict_lower(tm))


def _row_slab(ref, row):
    return ref.at[:, pl.ds(pl.multiple_of(row * TOK_ROWS, TOK_ROWS), TOK_ROWS), :]


def _wait_rows(hbm_ref, vmem_or_hbm_ref, sem, n_rows):
    n_sub = n_rows * TOK_ROWS
    pltpu.make_async_copy(hbm_ref.at[:, pl.ds(0, n_sub), :], vmem_or_hbm_ref.at[:, pl.ds(0, n_sub), :],
                          sem).wait()


def _dispatch_kernel(eid_ref, rank_ref, offs_ref, npk_ref, xg_in_ref, xg_ref, sem, *, tm):
    del xg_in_ref

    def issue(t, c):
        for k in range(TOP_K):
            j = t * TOP_K + k
            dst = offs_ref[eid_ref[j]] + rank_ref[j]
            pltpu.make_async_copy(_row_slab(npk_ref, t), _row_slab(xg_ref, dst), sem).start()
        return c

    lax.fori_loop(0, tm, issue, 0, unroll=4)
    _wait_rows(xg_ref, xg_ref, sem, tm * TOP_K)


def _dispatch(eid_flat, rank_flat, offs, npk, n_rows, tm):
    n_tok = npk.shape[1] // TOK_ROWS
    xg0 = jnp.zeros(_tok_shape(n_rows), F32)
    smem_blk = pl.BlockSpec((tm * TOP_K,), lambda i: (i,), memory_space=pltpu.SMEM)
    return pl.pallas_call(
        functools.partial(_dispatch_kernel, tm=tm),
        grid=(n_tok // tm,),
        in_specs=[smem_blk, smem_blk,
                  pl.BlockSpec(memory_space=pltpu.SMEM),
                  pl.BlockSpec(_tok_shape(tm), lambda i: (0, i, 0)),
                  pl.BlockSpec(memory_space=pl.ANY)],
        out_specs=pl.BlockSpec(memory_space=pl.ANY),
        out_shape=jax.ShapeDtypeStruct(_tok_shape(n_rows), F32),
        scratch_shapes=[pltpu.SemaphoreType.DMA(())],
        input_output_aliases={4: 0},
        compiler_params=_cparams(("arbitrary",)),
        name="moe_dispatch",
    )(eid_flat, rank_flat, offs, npk, xg0)


def _expert_kernel(te_ref, tb_ref, nu_ref, x_ref, wg_ref, bg_ref, wu_ref, bu_ref, wd_ref, bd_ref,
                   o_ref, xs_ref, acc_ref, *, n_fc, tr):
    del te_ref, tb_ref
    t = pl.program_id(0)
    f = pl.program_id(1)

    @pl.when(t < nu_ref[0])
    def _():
        @pl.when(f == 0)
        def _():
            for c in range(N_CHUNKS):
                xs_ref[:, c * LANES:(c + 1) * LANES] = x_ref[_slab_idx(0, tr, c)].astype(BF16)
            acc_ref[...] = jnp.zeros_like(acc_ref)

        x = xs_ref[...]
        gate = jnp.dot(x, wg_ref[0].astype(BF16), preferred_element_type=F32) + bg_ref[0]
        up = jnp.dot(x, wu_ref[0].astype(BF16), preferred_element_type=F32) + bu_ref[0]
        gate = jnp.minimum(gate, SWIGLU_LIMIT)
        up = jnp.clip(up, -SWIGLU_LIMIT, SWIGLU_LIMIT)
        act = (up + 1.0) * (gate * jax.nn.sigmoid(SWIGLU_ALPHA * gate))
        acc_ref[...] += jnp.dot(act.astype(BF16), wd_ref[0].astype(BF16), preferred_element_type=F32)

        @pl.when(f == n_fc - 1)
        def _():
            _store_slabs(acc_ref[...] + bd_ref[0], o_ref, tr)


def _experts(tile_expert, tile_block, n_used, xg, w_gate, b_gate, w_up, b_up, w_down, b_down, tr, fc):
    n_rows = xg.shape[1] // TOK_ROWS
    n_exp, d, dff = w_gate.shape
    n_fc = dff // fc
    grid_spec = pltpu.PrefetchScalarGridSpec(
        num_scalar_prefetch=3,
        grid=(n_rows // tr, n_fc),
        in_specs=[
            pl.BlockSpec(_tok_shape(tr), lambda t, f, te, tb, nu: (0, tb[t], 0)),
            pl.BlockSpec((1, d, fc), lambda t, f, te, tb, nu: (te[t], 0, f)),
            pl.BlockSpec((1, 1, fc), lambda t, f, te, tb, nu: (te[t], 0, f)),
            pl.BlockSpec((1, d, fc), lambda t, f, te, tb, nu: (te[t], 0, f)),
            pl.BlockSpec((1, 1, fc), lambda t, f, te, tb, nu: (te[t], 0, f)),
            pl.BlockSpec((1, fc, d), lambda t, f, te, tb, nu: (te[t], f, 0)),
            pl.BlockSpec((1, 1, d), lambda t, f, te, tb, nu: (te[t], 0, 0)),
        ],
        out_specs=pl.BlockSpec(_tok_shape(tr), lambda t, f, te, tb, nu: (0, tb[t], 0)),
        scratch_shapes=[pltpu.VMEM((tr, d), BF16), pltpu.VMEM((tr, d), F32)],
    )
    return pl.pallas_call(
        functools.partial(_expert_kernel, n_fc=n_fc, tr=tr),
        grid_spec=grid_spec,
        out_shape=jax.ShapeDtypeStruct(_tok_shape(n_rows), F32),
        input_output_aliases={3: 0},
        compiler_params=_cparams(("arbitrary", "arbitrary")),
        name="moe_experts",
    )(tile_expert, tile_block, n_used, xg, w_gate, b_gate.reshape(n_exp, 1, dff), w_up,
      b_up.reshape(n_exp, 1, dff), w_down, b_down.reshape(n_exp, 1, d))


def _combine_kernel(eid_ref, rank_ref, offs_ref, og_ref, meta_ref, h1_ref, gfin_ref, out_ref,
                    gbuf_ref, h2_ref, sem, *, tm):
    def issue(t, c):
        for k in range(TOP_K):
            j = t * TOP_K + k
            src = offs_ref[eid_ref[j]] + rank_ref[j]
            pltpu.make_async_copy(_row_slab(og_ref, src), _row_slab(gbuf_ref, k * tm + t), sem).start()
        return c

    lax.fori_loop(0, tm, issue, 0, unroll=4)
    _wait_rows(og_ref, gbuf_ref, sem, tm * TOP_K)

    wts = pltpu.bitcast(meta_ref[...], F32)
    wk = [wts[:, 2 * TOP_K + k:2 * TOP_K + k + 1] for k in range(TOP_K)]
    for c in range(N_CHUNKS):
        y = h1_ref[:, c * LANES:(c + 1) * LANES]
        for k in range(TOP_K):
            y = y + wk[k] * gbuf_ref[_slab_idx(k * tm, tm, c)]
        h2_ref[:, c * LANES:(c + 1) * LANES] = y
    h2 = h2_ref[...]
    ms = jnp.mean(h2 * h2, axis=-1, keepdims=True)
    out_ref[...] = h2 * lax.rsqrt(ms + RMS_EPS) * gfin_ref[...]


def _combine(eid_flat, rank_flat, offs, og, meta, h1, g_final, tm):
    n_tok = h1.shape[0]
    smem_blk = pl.BlockSpec((tm * TOP_K,), lambda i: (i,), memory_space=pltpu.SMEM)
    return pl.pallas_call(
        functools.partial(_combine_kernel, tm=tm),
        grid=(n_tok // tm,),
        in_specs=[smem_blk, smem_blk,
                  pl.BlockSpec(memory_space=pltpu.SMEM),
                  pl.BlockSpec(memory_space=pl.ANY),
                  pl.BlockSpec((tm, LANES), lambda i: (i, 0)),
                  pl.BlockSpec((tm, D_MODEL), lambda i: (i, 0)),
                  pl.BlockSpec((1, D_MODEL), lambda i: (0, 0))],
        out_specs=pl.BlockSpec((tm, D_MODEL), lambda i: (i, 0)),
        out_shape=jax.ShapeDtypeStruct((n_tok, D_MODEL), F32),
        scratch_shapes=[pltpu.VMEM(_tok_shape(tm * TOP_K), F32),
                        pltpu.VMEM((tm, D_MODEL), F32),
                        pltpu.SemaphoreType.DMA(())],
        compiler_params=_cparams(("arbitrary",)),
        name="moe_combine",
    )(eid_flat, rank_flat, offs, og, meta, h1, g_final.reshape(1, D_MODEL))


def _routing_tables(counts, tr, n_tiles):
    ntile = (counts + tr - 1) // tr
    tiles_cum = jnp.cumsum(ntile)
    offs = ((tiles_cum - ntile) * tr).astype(I32)
    n_used = tiles_cum[-1]
    t = jnp.minimum(jnp.arange(n_tiles, dtype=I32), n_used - 1)
    tile_expert = jnp.sum((tiles_cum[None, :] <= t[:, None]).astype(I32), axis=1)
    return offs, tile_expert, t.astype(I32), n_used.reshape(1).astype(I32)


def kernel(x, meta_tokens, g_mix, w_in, conv_w, w_conv_out, w_attn_out, w_o, g_ffn, w_router,
           b_router, w_gate, b_gate, w_up, b_up, w_down, b_down, g_final):
    assert g_mix.shape[0] == 1, "single-layer trunk"
    bsz, seq, d = x.shape
    n_tok = bsz * seq
    x2d = x.reshape(n_tok, d)

    w_in_bf = w_in[0].astype(BF16)
    hn = _rmsnorm_bf16(x2d, g_mix[0], RMS_TM)
    proj = _matmul_bf16(hn, w_in_bf, PROJ_TM, PROJ_TN)
    meta_pad = jnp.pad(meta_tokens.astype(x.dtype), ((META_ROWS - N_META, 0), (0, 0)))
    hn_meta = _rmsnorm_bf16(meta_pad, g_mix[0], META_ROWS)
    proj_meta = _matmul_bf16(hn_meta, w_in_bf, META_ROWS, PROJ_TN)

    attn_o = _attention(proj, proj_meta, bsz, seq, ATT_TQ, ATT_TK)

    wr = jnp.pad(w_router[0], ((0, 0), (0, LANES - N_EXPERTS)))
    wr_hi = wr.astype(BF16)
    wr_lo = (wr - wr_hi.astype(F32)).astype(BF16)
    b_r = jnp.pad(b_router[0], (0, LANES - N_EXPERTS)).reshape(1, LANES)
    h1, npk, meta, cnt = _mixer(proj, proj_meta, attn_o, x2d, conv_w[0],
                                w_conv_out[0].astype(BF16), w_attn_out[0].astype(BF16),
                                w_o[0].astype(BF16), g_ffn[0], wr_hi, wr_lo, b_r, seq, MIX_TM)

    n_tiles = (n_tok * TOP_K) // EXPERT_TR + N_EXPERTS
    offs, tile_expert, tile_block, n_used = _routing_tables(cnt[0, :N_EXPERTS], EXPERT_TR, n_tiles)
    eid_flat = meta[:, 0:TOP_K].reshape(-1)
    rank_flat = meta[:, TOP_K:2 * TOP_K].reshape(-1)

    xg = _dispatch(eid_flat, rank_flat, offs, npk, n_tiles * EXPERT_TR, DISPATCH_TM)
    og = _experts(tile_expert, tile_block, n_used, xg, w_gate[0], b_gate[0], w_up[0], b_up[0],
                  w_down[0], b_down[0], EXPERT_TR, EXPERT_FC)
    out = _combine(eid_flat, rank_flat, offs, og, meta, h1, g_final, COMBINE_TM)
    return out.reshape(bsz, seq, d)
```

```python
import functools
import math

import jax
import jax.numpy as jnp
from jax import lax
from jax.experimental import pallas as pl
from jax.experimental.pallas import tpu as pltpu

F32 = jnp.float32
BF16 = jnp.bfloat16
I32 = jnp.int32

D_MODEL = 2048
N_META = 16
N_HEADS = 8
HEAD_DIM = 128
ATTN_WIDTH = N_HEADS * HEAD_DIM
CONV_CH = D_MODEL // 2
CONV_K = 3
N_EXPERTS = 32
TOP_K = 4
D_FF = D_MODEL
SWIGLU_LIMIT = 7.0
SWIGLU_ALPHA = 1.702
RMS_EPS = 1e-5
IN_WIDTH = 3 * CONV_CH + 3 * ATTN_WIDTH + 2 * D_MODEL

LANES = 128
SUBLANES = 8
META_ROWS = 128
LOG2E = 1.4426950408889634
VMEM_LIMIT = 56 * 1024 * 1024

RMS_TM = 512
PROJ_TM = 1024
PROJ_TN = 1024
ATT_TQ = 512
ATT_TK = 256
MIX_TM = 256
DISPATCH_TM = 256
EXPERT_TR = 768
EXPERT_FC = 512
COMBINE_TM = 128
TOK_ROWS = SUBLANES
TOK_HALVES = D_MODEL // (TOK_ROWS * LANES)


def _tok_shape(n_rows):
    return (TOK_HALVES, n_rows * TOK_ROWS, LANES)


def _cparams(sem, vmem=VMEM_LIMIT):
    return pltpu.CompilerParams(dimension_semantics=sem, vmem_limit_bytes=vmem)


def _rmsnorm_kernel(x_ref, g_ref, o_ref):
    x = x_ref[...].astype(F32)
    ms = jnp.mean(x * x, axis=-1, keepdims=True)
    o_ref[...] = (x * lax.rsqrt(ms + RMS_EPS) * g_ref[...]).astype(o_ref.dtype)


def _rmsnorm_bf16(x, g, tm):
    m, d = x.shape
    return pl.pallas_call(
        _rmsnorm_kernel,
        grid=(m // tm,),
        in_specs=[pl.BlockSpec((tm, d), lambda i: (i, 0)),
                  pl.BlockSpec((1, d), lambda i: (0, 0))],
        out_specs=pl.BlockSpec((tm, d), lambda i: (i, 0)),
        out_shape=jax.ShapeDtypeStruct((m, d), BF16),
        compiler_params=_cparams(("arbitrary",)),
        name="rmsnorm",
    )(x, g.reshape(1, d))


ZSCALE = LOG2E / math.sqrt(HEAD_DIM)
Q_COL0 = 3 * CONV_CH


def _in_proj_kernel(x_ref, w_ref, o_ref, *rest, q_tile, zero_fill):
    wbf_ref = rest[-1]
    j = pl.program_id(0)

    @pl.when(pl.program_id(1) == 0)
    def _():
        wbf_ref[...] = w_ref[...].astype(BF16)

    acc = jnp.dot(x_ref[...], wbf_ref[...], preferred_element_type=F32)
    o_ref[...] = (acc * jnp.where(j == q_tile, ZSCALE, 1.0)).astype(o_ref.dtype)
    if zero_fill:
        rest[0][...] = jnp.zeros_like(rest[0])


def _in_proj(x, w, tm, tn, zero_rows=0):
    m, k = x.shape
    _, n = w.shape
    assert Q_COL0 % tn == 0 and ATTN_WIDTH == tn
    steps = (n // tn) * (m // tm)
    in_specs = [pl.BlockSpec((tm, k), lambda j, i: (i, 0)),
                pl.BlockSpec((k, tn), lambda j, i: (0, j))]
    out_specs = [pl.BlockSpec((tm, tn), lambda j, i: (i, j))]
    out_shape = [jax.ShapeDtypeStruct((m, n), BF16)]
    if zero_rows:
        nblk = max(d for d in range(1, steps + 1) if zero_rows % d == 0)
        n_i = m // tm
        out_specs.append(pl.BlockSpec(_tok_shape(zero_rows // nblk),
                                      lambda j, i: (0, jnp.minimum(j * n_i + i, nblk - 1), 0)))
        out_shape.append(jax.ShapeDtypeStruct(_tok_shape(zero_rows), F32))
    out = pl.pallas_call(
        functools.partial(_in_proj_kernel, q_tile=Q_COL0 // tn, zero_fill=bool(zero_rows)),
        grid=(n // tn, m // tm),
        in_specs=in_specs,
        out_specs=out_specs,
        out_shape=out_shape,
        scratch_shapes=[pltpu.VMEM((k, tn), BF16)],
        compiler_params=_cparams(("arbitrary", "arbitrary")),
        name="in_proj",
    )(x, w)
    return out if zero_rows else out[0]


def _attn_kernel(q_ref, k_ref, v_ref, km_ref, vm_ref, u_ref, um_ref, wg_ref, wu_ref, wd_ref,
                 o_ref, wgo_ref, wuo_ref, wdo_ref, acc_ref, r_ref, *, tq, tk, fc):
    for c in range(D_FF // fc):
        wgo_ref[0, c] = wg_ref[:, c * fc:(c + 1) * fc].astype(BF16)
        wuo_ref[0, c] = wu_ref[:, c * fc:(c + 1) * fc].astype(BF16)
    wdo_ref[...] = wd_ref[...].astype(BF16)

    qi = pl.program_id(2)
    q0 = qi * tq
    n_sub = tq // tk
    acc_ref[...] = jnp.zeros_like(acc_ref)
    r_ref[...] = jnp.zeros_like(r_ref)

    def chain(k, v, u, mask):
        z2 = lax.dot_general(q_ref[0], k, (((1,), (1,)), ((), ())), preferred_element_type=F32)
        e = jnp.exp2(-jnp.abs(z2))
        s2 = jnp.maximum(z2, 0.0) + jnp.log2(1.0 + e)
        if mask is not None:
            s2 = jnp.where(mask, s2, 0.0)
        cum = jnp.dot(s2.astype(BF16), u, preferred_element_type=F32)
        w = jnp.exp2(z2 - cum - r_ref[...])
        if mask is not None:
            w = jnp.where(mask, w, 0.0)
        acc_ref[...] += jnp.dot(w.astype(BF16), v, preferred_element_type=F32)
        r_ref[...] += cum[:, 0:1]

    u = u_ref[...]
    row = lax.broadcasted_iota(I32, (tq, tk), 0)
    col = lax.broadcasted_iota(I32, (tq, tk), 1)
    for d in reversed(range(n_sub)):
        k0 = pl.multiple_of(q0 + d * tk, tk)
        chain(k_ref[0, pl.ds(k0, tk), :], v_ref[0, pl.ds(k0, tk), :], u, col + d * tk < row)

    def full(jj, c):
        for d in reversed(range(n_sub)):
            k0 = pl.multiple_of(((qi - 1 - jj) * n_sub + d) * tk, tk)
            chain(k_ref[0, pl.ds(k0, tk), :], v_ref[0, pl.ds(k0, tk), :], u, None)
        return c

    lax.fori_loop(0, qi, full, 0)

    valid_meta = lax.broadcasted_iota(I32, (tq, META_ROWS), 1) >= META_ROWS - N_META
    chain(km_ref[...], vm_ref[...], um_ref[...], valid_meta)
    o_ref[0] = acc_ref[...].astype(o_ref.dtype)


def _tri_incl(n):
    r = lax.broadcasted_iota(I32, (n, n), 0)
    c = lax.broadcasted_iota(I32, (n, n), 1)
    return (r >= c).astype(BF16)


def _attention(proj, proj_meta, w_gate, w_up, w_down, bsz, seq, tq, tk, fc):
    proj3 = proj.reshape(bsz, seq, IN_WIDTH)
    qb = Q_COL0 // HEAD_DIM
    kb, vb = qb + N_HEADS, qb + 2 * N_HEADS
    n_q = seq // tq
    n_exp, d, dff = w_gate.shape
    n_fc = dff // fc
    steps = bsz * N_HEADS * n_q
    rc = (n_exp * d) // steps
    assert dff == d and rc * steps == n_exp * d and d % rc == 0 and rc % (2 * SUBLANES) == 0
    blk_per_e = d // rc

    def step(b, h, qi):
        return (b * N_HEADS + h) * n_q + qi

    w_in_spec = pl.BlockSpec((rc, dff), lambda b, h, qi: (step(b, h, qi), 0))
    wgu_out_spec = pl.BlockSpec((1, n_fc, rc, fc),
                                lambda b, h, qi: (step(b, h, qi) // blk_per_e, 0, step(b, h, qi) % blk_per_e, 0))
    out, wg_bf, wu_bf, wd_bf = pl.pallas_call(
        functools.partial(_attn_kernel, tq=tq, tk=tk, fc=fc),
        grid=(bsz, N_HEADS, n_q),
        in_specs=[pl.BlockSpec((1, tq, HEAD_DIM), lambda b, h, qi: (b, qi, qb + h)),
                  pl.BlockSpec((1, seq, HEAD_DIM), lambda b, h, qi: (b, 0, kb + h)),
                  pl.BlockSpec((1, seq, HEAD_DIM), lambda b, h, qi: (b, 0, vb + h)),
                  pl.BlockSpec((META_ROWS, HEAD_DIM), lambda b, h, qi: (0, kb + h)),
                  pl.BlockSpec((META_ROWS, HEAD_DIM), lambda b, h, qi: (0, vb + h)),
                  pl.BlockSpec((tk, tk), lambda b, h, qi: (0, 0)),
                  pl.BlockSpec((META_ROWS, META_ROWS), lambda b, h, qi: (0, 0)),
                  w_in_spec, w_in_spec,
                  pl.BlockSpec((rc, d), lambda b, h, qi: (step(b, h, qi), 0))],
        out_specs=[pl.BlockSpec((1, tq, HEAD_DIM), lambda b, h, qi: (b, qi, h)),
                   wgu_out_spec, wgu_out_spec,
                   pl.BlockSpec((rc, d), lambda b, h, qi: (step(b, h, qi), 0))],
        out_shape=[jax.ShapeDtypeStruct((bsz, seq, ATTN_WIDTH), BF16),
                   jax.ShapeDtypeStruct((n_exp, n_fc, d, fc), BF16),
                   jax.ShapeDtypeStruct((n_exp, n_fc, d, fc), BF16),
                   jax.ShapeDtypeStruct((n_exp * dff, d), BF16)],
        scratch_shapes=[pltpu.VMEM((tq, HEAD_DIM), F32), pltpu.VMEM((tq, 1), F32)],
        compiler_params=_cparams(("arbitrary", "arbitrary", "arbitrary")),
        name="stickbreak_attn",
    )(proj3, proj3, proj3, proj_meta, proj_meta, _tri_incl(tk), _tri_incl(META_ROWS),
      w_gate.reshape(n_exp * d, dff), w_up.reshape(n_exp * d, dff), w_down.reshape(n_exp * dff, d))
    return out.reshape(bsz * seq, ATTN_WIDTH), wg_bf, wu_bf, wd_bf.reshape(n_exp, dff, d)


def _slab_idx(first_tok, n_tok, c):
    h, s = divmod(c, TOK_ROWS)
    return (h, pl.ds(first_tok * TOK_ROWS + s, n_tok, stride=TOK_ROWS), slice(None))


N_CHUNKS = D_MODEL // LANES


def _store_slabs(vals, out_ref, n_tok):
    for c in range(N_CHUNKS):
        out_ref[_slab_idx(0, n_tok, c)] = vals[:, c * LANES:(c + 1) * LANES]


def _mixer_kernel(u_ref, bp_ref, cp_ref, gc_ref, ga_ref, o_ref, x_ref, um_ref, cm_ref,
                  convw_ref, wc_ref, wa_ref, wo_ref, gffn_ref, wrh_ref, wrl_ref, br_ref, ltri_ref,
                  h1_ref, npk_ref, meta_ref, wts_ref, cnt_ref,
                  cu_ref, carry_ref, *, tm, tiles_per_seq):
    i = pl.program_id(0)
    first = (i % tiles_per_seq) == 0

    @pl.when(i == 0)
    def _():
        carry_ref[...] = jnp.zeros_like(carry_ref)

    @pl.when(first)
    def _():
        cum = cm_ref[...].astype(F32) * um_ref[...].astype(F32)
        cu_ref[0:SUBLANES, :] = cum[SUBLANES:2 * SUBLANES, :]

    @pl.when(jnp.logical_not(first))
    def _():
        cu_ref[0:SUBLANES, :] = cu_ref[tm:tm + SUBLANES, :]

    cu = cp_ref[...].astype(F32) * u_ref[...].astype(F32)
    cu_ref[SUBLANES:tm + SUBLANES, :] = cu
    cw = convw_ref[...]
    conv = (cu_ref[SUBLANES - 2:tm + SUBLANES - 2, :] * cw[0:1, :]
            + cu_ref[SUBLANES - 1:tm + SUBLANES - 1, :] * cw[1:2, :]
            + cu * cw[2:3, :])
    y_conv = jnp.dot((bp_ref[...].astype(F32) * conv).astype(BF16), wc_ref[...],
                     preferred_element_type=F32)
    y_attn = jnp.dot(o_ref[...], wa_ref[...], preferred_element_type=F32)
    merged = (jax.nn.sigmoid(gc_ref[...].astype(F32)) * y_conv
              + jax.nn.sigmoid(ga_ref[...].astype(F32)) * y_attn)
    h1 = x_ref[...] + jnp.dot(merged.astype(BF16), wo_ref[...], preferred_element_type=F32)
    h1_ref[...] = h1

    ms = jnp.mean(h1 * h1, axis=-1, keepdims=True)
    n = h1 * lax.rsqrt(ms + RMS_EPS) * gffn_ref[...]
    _store_slabs(n, npk_ref, tm)

    n_hi = n.astype(BF16)
    n_lo = (n - n_hi.astype(F32)).astype(BF16)
    logits = (jnp.dot(n_hi, wrh_ref[...], preferred_element_type=F32)
              + jnp.dot(n_lo, wrh_ref[...], preferred_element_type=F32)
              + jnp.dot(n_hi, wrl_ref[...], preferred_element_type=F32)) + br_ref[...]
    lane = lax.broadcasted_iota(I32, (tm, LANES), 1)
    lg = jnp.where(lane < N_EXPERTS, logits, -jnp.inf)

    sels, tops, idxs = [], [], []
    for _ in range(TOP_K):
        m = jnp.max(lg, axis=-1, keepdims=True)
        idx = jnp.min(jnp.where(lg == m, lane, LANES), axis=-1, keepdims=True)
        sel = lane == idx
        sels.append(sel)
        tops.append(m)
        idxs.append(idx)
        lg = jnp.where(sel, -jnp.inf, lg)
    exps = [jnp.exp(t - tops[0]) for t in tops]
    denom = exps[0] + exps[1] + exps[2] + exps[3]
    wts = [e / denom for e in exps]

    onehot = jnp.zeros((tm, LANES), F32)
    for sel in sels:
        onehot = onehot + sel.astype(F32)
    base = carry_ref[0:1, :] + jnp.dot(ltri_ref[...], onehot.astype(BF16), preferred_element_type=F32)
    meta = jnp.zeros((tm, LANES), I32)
    wlanes = jnp.zeros((tm, LANES), F32)
    for k in range(TOP_K):
        rank = jnp.sum(jnp.where(sels[k], base, 0.0), axis=-1, keepdims=True)
        meta = jnp.where(lane == k, idxs[k], meta)
        meta = jnp.where(lane == TOP_K + k, rank.astype(I32), meta)
        wlanes = jnp.where(lane == k, wts[k], wlanes)
    meta_ref[...] = meta
    wts_ref[...] = wlanes
    carry_ref[0:1, :] = carry_ref[0:1, :] + jnp.sum(onehot, axis=0, keepdims=True)
    cnt_ref[...] = jnp.broadcast_to(carry_ref[0:1, :], cnt_ref.shape).astype(I32)


def _tri_strict_lower(n):
    r = lax.broadcasted_iota(I32, (n, n), 0)
    c = lax.broadcasted_iota(I32, (n, n), 1)
    return (c < r).astype(BF16)


def _const_spec(shape):
    return pl.BlockSpec(shape, lambda i: (0,) * len(shape))


def _mixer(proj, proj_meta, attn_o, x2d, conv_w, wc, wa, wo, g_ffn, wr_hi, wr_lo, b_r, seq, tm):
    n_tok = x2d.shape[0]
    tiles_per_seq = seq // tm
    meta_blk = META_ROWS // (2 * SUBLANES) - 1
    in_specs = [
        pl.BlockSpec((tm, CONV_CH), lambda i: (i, 0)),
        pl.BlockSpec((tm, CONV_CH), lambda i: (i, 1)),
        pl.BlockSpec((tm, CONV_CH), lambda i: (i, 2)),
        pl.BlockSpec((tm, D_MODEL), lambda i: (i, 3)),
        pl.BlockSpec((tm, D_MODEL), lambda i: (i, 4)),
        pl.BlockSpec((tm, ATTN_WIDTH), lambda i: (i, 0)),
        pl.BlockSpec((tm, D_MODEL), lambda i: (i, 0)),
        pl.BlockSpec((2 * SUBLANES, CONV_CH), lambda i: (meta_blk, 0)),
        pl.BlockSpec((2 * SUBLANES, CONV_CH), lambda i: (meta_blk, 2)),
        _const_spec((SUBLANES, CONV_CH)),
        _const_spec((CONV_CH, D_MODEL)),
        _const_spec((ATTN_WIDTH, D_MODEL)),
        _const_spec((D_MODEL, D_MODEL)),
        _const_spec((1, D_MODEL)),
        _const_spec((D_MODEL, LANES)),
        _const_spec((D_MODEL, LANES)),
        _const_spec((1, LANES)),
        _const_spec((tm, tm)),
    ]
    out_specs = [
        pl.BlockSpec((tm, D_MODEL), lambda i: (i, 0)),
        pl.BlockSpec(_tok_shape(tm), lambda i: (0, i, 0)),
        pl.BlockSpec((tm, LANES), lambda i: (i, 0)),
        pl.BlockSpec((tm, LANES), lambda i: (i, 0)),
        _const_spec((SUBLANES, LANES)),
    ]
    out_shape = [
        jax.ShapeDtypeStruct((n_tok, D_MODEL), F32),
        jax.ShapeDtypeStruct(_tok_shape(n_tok), F32),
        jax.ShapeDtypeStruct((n_tok, LANES), I32),
        jax.ShapeDtypeStruct((n_tok, LANES), F32),
        jax.ShapeDtypeStruct((SUBLANES, LANES), I32),
    ]
    conv_w8 = jnp.pad(conv_w, ((0, SUBLANES - CONV_K), (0, 0)))
    return pl.pallas_call(
        functools.partial(_mixer_kernel, tm=tm, tiles_per_seq=tiles_per_seq),
        grid=(n_tok // tm,),
        in_specs=in_specs,
        out_specs=out_specs,
        out_shape=out_shape,
        scratch_shapes=[pltpu.VMEM((tm + SUBLANES, CONV_CH), F32), pltpu.VMEM((SUBLANES, LANES), F32)],
        compiler_params=_cparams(("arbitrary",)),
        name="mixer_out",
    )(proj, proj, proj, proj, proj, attn_o, x2d, proj_meta, proj_meta, conv_w8, wc, wa, wo,
      g_ffn.reshape(1, D_MODEL), wr_hi, wr_lo, b_r, _tri_strict_lower(tm))


def _row_slab(ref, row):
    return ref.at[:, pl.ds(pl.multiple_of(row * TOK_ROWS, TOK_ROWS), TOK_ROWS), :]


def _wait_rows(hbm_ref, vmem_or_hbm_ref, sem, n_rows):
    n_sub = n_rows * TOK_ROWS
    pltpu.make_async_copy(hbm_ref.at[:, pl.ds(0, n_sub), :], vmem_or_hbm_ref.at[:, pl.ds(0, n_sub), :],
                          sem).wait()


def _dispatch_kernel(eid_ref, rank_ref, offs_ref, npk_ref, xg_in_ref, xg_ref, sem, *, tm):
    del xg_in_ref

    def issue(t, c):
        for k in range(TOP_K):
            j = t * TOP_K + k
            dst = offs_ref[eid_ref[j]] + rank_ref[j]
            pltpu.make_async_copy(_row_slab(npk_ref, t), _row_slab(xg_ref, dst), sem).start()
        return c

    lax.fori_loop(0, tm, issue, 0, unroll=4)
    _wait_rows(xg_ref, xg_ref, sem, tm * TOP_K)


def _dispatch(eid_flat, rank_flat, offs, npk, xg0, tm):
    n_tok = npk.shape[1] // TOK_ROWS
    n_rows = xg0.shape[1] // TOK_ROWS
    smem_blk = pl.BlockSpec((tm * TOP_K,), lambda i: (i,), memory_space=pltpu.SMEM)
    return pl.pallas_call(
        functools.partial(_dispatch_kernel, tm=tm),
        grid=(n_tok // tm,),
        in_specs=[smem_blk, smem_blk,
                  pl.BlockSpec(memory_space=pltpu.SMEM),
                  pl.BlockSpec(_tok_shape(tm), lambda i: (0, i, 0)),
                  pl.BlockSpec(memory_space=pl.ANY)],
        out_specs=pl.BlockSpec(memory_space=pl.ANY),
        out_shape=jax.ShapeDtypeStruct(_tok_shape(n_rows), F32),
        scratch_shapes=[pltpu.SemaphoreType.DMA(())],
        input_output_aliases={4: 0},
        compiler_params=_cparams(("arbitrary",)),
        name="moe_dispatch",
    )(eid_flat, rank_flat, offs, npk, xg0)


def _expert_kernel(te_ref, tb_ref, nu_ref, x_ref, wg_ref, bg_ref, wu_ref, bu_ref, wd_ref, bd_ref,
                   o_ref, xs_ref, acc_ref, *, n_fc, tr):
    del te_ref, tb_ref
    t = pl.program_id(0)
    f = pl.program_id(1)

    @pl.when(t < nu_ref[0])
    def _():
        @pl.when(f == 0)
        def _():
            for c in range(N_CHUNKS):
                xs_ref[:, c * LANES:(c + 1) * LANES] = x_ref[_slab_idx(0, tr, c)].astype(BF16)
            acc_ref[...] = jnp.zeros_like(acc_ref)

        x = xs_ref[...]
        gate = jnp.dot(x, wg_ref[0, 0], preferred_element_type=F32) + bg_ref[0]
        up = jnp.dot(x, wu_ref[0, 0], preferred_element_type=F32) + bu_ref[0]
        gate = jnp.minimum(gate, SWIGLU_LIMIT)
        up = jnp.clip(up, -SWIGLU_LIMIT, SWIGLU_LIMIT)
        act = (up + 1.0) * (gate * jax.nn.sigmoid(SWIGLU_ALPHA * gate))
        acc_ref[...] += jnp.dot(act.astype(BF16), wd_ref[0], preferred_element_type=F32)

        @pl.when(f == n_fc - 1)
        def _():
            _store_slabs(acc_ref[...] + bd_ref[0], o_ref, tr)


def _experts(tile_expert, tile_block, n_used, xg, w_gate, b_gate, w_up, b_up, w_down, b_down, tr):
    n_rows = xg.shape[1] // TOK_ROWS
    n_exp, n_fc, d, fc = w_gate.shape
    dff = n_fc * fc
    grid_spec = pltpu.PrefetchScalarGridSpec(
        num_scalar_prefetch=3,
        grid=(n_rows // tr, n_fc),
        in_specs=[
            pl.BlockSpec(_tok_shape(tr), lambda t, f, te, tb, nu: (0, tb[t], 0)),
            pl.BlockSpec((1, 1, d, fc), lambda t, f, te, tb, nu: (te[t], f, 0, 0)),
            pl.BlockSpec((1, 1, fc), lambda t, f, te, tb, nu: (te[t], 0, f)),
            pl.BlockSpec((1, 1, d, fc), lambda t, f, te, tb, nu: (te[t], f, 0, 0)),
            pl.BlockSpec((1, 1, fc), lambda t, f, te, tb, nu: (te[t], 0, f)),
            pl.BlockSpec((1, fc, d), lambda t, f, te, tb, nu: (te[t], f, 0)),
            pl.BlockSpec((1, 1, d), lambda t, f, te, tb, nu: (te[t], 0, 0)),
        ],
        out_specs=pl.BlockSpec(_tok_shape(tr), lambda t, f, te, tb, nu: (0, tb[t], 0)),
        scratch_shapes=[pltpu.VMEM((tr, d), BF16), pltpu.VMEM((tr, d), F32)],
    )
    return pl.pallas_call(
        functools.partial(_expert_kernel, n_fc=n_fc, tr=tr),
        grid_spec=grid_spec,
        out_shape=jax.ShapeDtypeStruct(_tok_shape(n_rows), F32),
        input_output_aliases={3: 0},
        compiler_params=_cparams(("arbitrary", "arbitrary")),
        name="moe_experts",
    )(tile_expert, tile_block, n_used, xg, w_gate, b_gate.reshape(n_exp, 1, dff), w_up,
      b_up.reshape(n_exp, 1, dff), w_down, b_down.reshape(n_exp, 1, d))


def _combine_kernel(eid_ref, rank_ref, offs_ref, og_ref, wts_ref, h1_ref, gfin_ref, out_ref,
                    gbuf_ref, h2_ref, sem, *, tm):
    def issue(t, c):
        for k in range(TOP_K):
            j = t * TOP_K + k
            src = offs_ref[eid_ref[j]] + rank_ref[j]
            pltpu.make_async_copy(_row_slab(og_ref, src), _row_slab(gbuf_ref, k * tm + t), sem).start()
        return c

    lax.fori_loop(0, tm, issue, 0, unroll=4)
    _wait_rows(og_ref, gbuf_ref, sem, tm * TOP_K)

    wts = wts_ref[...]
    wk = [wts[:, k:k + 1] for k in range(TOP_K)]
    for c in range(N_CHUNKS):
        y = h1_ref[:, c * LANES:(c + 1) * LANES]
        for k in range(TOP_K):
            y = y + wk[k] * gbuf_ref[_slab_idx(k * tm, tm, c)]
        h2_ref[:, c * LANES:(c + 1) * LANES] = y
    h2 = h2_ref[...]
    ms = jnp.mean(h2 * h2, axis=-1, keepdims=True)
    out_ref[...] = h2 * lax.rsqrt(ms + RMS_EPS) * gfin_ref[...]


def _combine(eid_flat, rank_flat, offs, og, wts, h1, g_final, tm):
    n_tok = h1.shape[0]
    smem_blk = pl.BlockSpec((tm * TOP_K,), lambda i: (i,), memory_space=pltpu.SMEM)
    return pl.pallas_call(
        functools.partial(_combine_kernel, tm=tm),
        grid=(n_tok // tm,),
        in_specs=[smem_blk, smem_blk,
                  pl.BlockSpec(memory_space=pltpu.SMEM),
                  pl.BlockSpec(memory_space=pl.ANY),
                  pl.BlockSpec((tm, LANES), lambda i: (i, 0)),
                  pl.BlockSpec((tm, D_MODEL), lambda i: (i, 0)),
                  pl.BlockSpec((1, D_MODEL), lambda i: (0, 0))],
        out_specs=pl.BlockSpec((tm, D_MODEL), lambda i: (i, 0)),
        out_shape=jax.ShapeDtypeStruct((n_tok, D_MODEL), F32),
        scratch_shapes=[pltpu.VMEM(_tok_shape(tm * TOP_K), F32),
                        pltpu.VMEM((tm, D_MODEL), F32),
                        pltpu.SemaphoreType.DMA(())],
        compiler_params=_cparams(("arbitrary",)),
        name="moe_combine",
    )(eid_flat, rank_flat, offs, og, wts, h1, g_final.reshape(1, D_MODEL))


def _routing_tables(counts, tr, n_tiles):
    ntile = (counts + tr - 1) // tr
    tiles_cum = jnp.cumsum(ntile)
    offs = ((tiles_cum - ntile) * tr).astype(I32)
    n_used = tiles_cum[-1]
    t = jnp.minimum(jnp.arange(n_tiles, dtype=I32), n_used - 1)
    tile_expert = jnp.sum((tiles_cum[None, :] <= t[:, None]).astype(I32), axis=1)
    return offs, tile_expert, t.astype(I32), n_used.reshape(1).astype(I32)


def kernel(x, meta_tokens, g_mix, w_in, conv_w, w_conv_out, w_attn_out, w_o, g_ffn, w_router,
           b_router, w_gate, b_gate, w_up, b_up, w_down, b_down, g_final):
    assert g_mix.shape[0] == 1, "single-layer trunk"
    bsz, seq, d = x.shape
    n_tok = bsz * seq
    x2d = x.reshape(n_tok, d)

    n_tiles = (n_tok * TOP_K) // EXPERT_TR + N_EXPERTS
    hn = _rmsnorm_bf16(x2d, g_mix[0], RMS_TM)
    proj, xg0 = _in_proj(hn, w_in[0], PROJ_TM, PROJ_TN, zero_rows=n_tiles * EXPERT_TR)
    meta_pad = jnp.pad(meta_tokens.astype(x.dtype), ((META_ROWS - N_META, 0), (0, 0)))
    hn_meta = _rmsnorm_bf16(meta_pad, g_mix[0], META_ROWS)
    proj_meta = _in_proj(hn_meta, w_in[0], META_ROWS, PROJ_TN)

    attn_o, wg_bf, wu_bf, wd_bf = _attention(proj, proj_meta, w_gate[0], w_up[0], w_down[0],
                                             bsz, seq, ATT_TQ, ATT_TK, EXPERT_FC)

    wr = jnp.pad(w_router[0], ((0, 0), (0, LANES - N_EXPERTS)))
    wr_hi = wr.astype(BF16)
    wr_lo = (wr - wr_hi.astype(F32)).astype(BF16)
    b_r = jnp.pad(b_router[0], (0, LANES - N_EXPERTS)).reshape(1, LANES)
    h1, npk, meta, wts, cnt = _mixer(proj, proj_meta, attn_o, x2d, conv_w[0],
                                     w_conv_out[0].astype(BF16), w_attn_out[0].astype(BF16),
                                     w_o[0].astype(BF16), g_ffn[0], wr_hi, wr_lo, b_r, seq, MIX_TM)

    offs, tile_expert, tile_block, n_used = _routing_tables(cnt[0, :N_EXPERTS], EXPERT_TR, n_tiles)
    eid_flat = meta[:, 0:TOP_K].reshape(-1)
    rank_flat = meta[:, TOP_K:2 * TOP_K].reshape(-1)

    xg = _dispatch(eid_flat, rank_flat, offs, npk, xg0, DISPATCH_TM)
    og = _experts(tile_expert, tile_block, n_used, xg, wg_bf, b_gate[0], wu_bf, b_up[0],
                  wd_bf, b_down[0], EXPERT_TR)
    out = _combine(eid_flat, rank_flat, offs, og, wts, h1, g_final, COMBINE_TM)
    return out.reshape(bsz, seq, d)
```

```python
import functools
import math

import jax
import jax.numpy as jnp
from jax import lax
from jax.experimental import pallas as pl
from jax.experimental.pallas import tpu as pltpu

F32 = jnp.float32
BF16 = jnp.bfloat16
I32 = jnp.int32

D_MODEL = 2048
N_META = 16
N_HEADS = 8
HEAD_DIM = 128
ATTN_WIDTH = N_HEADS * HEAD_DIM
CONV_CH = D_MODEL // 2
CONV_K = 3
N_EXPERTS = 32
TOP_K = 4
D_FF = D_MODEL
SWIGLU_LIMIT = 7.0
SWIGLU_ALPHA = 1.702
RMS_EPS = 1e-5
IN_WIDTH = 3 * CONV_CH + 3 * ATTN_WIDTH + 2 * D_MODEL

LANES = 128
SUBLANES = 8
META_ROWS = 128
LOG2E = 1.4426950408889634
VMEM_LIMIT = 56 * 1024 * 1024

RMS_TM = 512
PROJ_TM = 1024
PROJ_TN = 1024
ATT_TQ = 512
ATT_TK = 256
MIX_TM = 256
DISPATCH_TM = 256
EXPERT_TR = 768
EXPERT_FC = 512
COMBINE_TM = 128
TOK_ROWS = SUBLANES
TOK_HALVES = D_MODEL // (TOK_ROWS * LANES)


def _tok_shape(n_rows):
    return (TOK_HALVES, n_rows * TOK_ROWS, LANES)


def _cparams(sem, vmem=VMEM_LIMIT):
    return pltpu.CompilerParams(dimension_semantics=sem, vmem_limit_bytes=vmem)


def _rmsnorm_kernel(x_ref, g_ref, o_ref):
    x = x_ref[...].astype(F32)
    ms = jnp.mean(x * x, axis=-1, keepdims=True)
    o_ref[...] = (x * lax.rsqrt(ms + RMS_EPS) * g_ref[...]).astype(o_ref.dtype)


def _rmsnorm_bf16(x, g, tm):
    m, d = x.shape
    return pl.pallas_call(
        _rmsnorm_kernel,
        grid=(m // tm,),
        in_specs=[pl.BlockSpec((tm, d), lambda i: (i, 0)),
                  pl.BlockSpec((1, d), lambda i: (0, 0))],
        out_specs=pl.BlockSpec((tm, d), lambda i: (i, 0)),
        out_shape=jax.ShapeDtypeStruct((m, d), BF16),
        compiler_params=_cparams(("arbitrary",)),
        name="rmsnorm",
    )(x, g.reshape(1, d))


ZSCALE = LOG2E / math.sqrt(HEAD_DIM)
Q_COL0 = 3 * CONV_CH


def _in_proj_kernel(x_ref, w_ref, o_ref, *rest, q_tile, zero_fill):
    wbf_ref = rest[-1]
    j = pl.program_id(0)

    @pl.when(pl.program_id(1) == 0)
    def _():
        wbf_ref[...] = w_ref[...].astype(BF16)

    acc = jnp.dot(x_ref[...], wbf_ref[...], preferred_element_type=F32)
    o_ref[...] = (acc * jnp.where(j == q_tile, ZSCALE, 1.0)).astype(o_ref.dtype)
    if zero_fill:
        rest[0][...] = jnp.zeros_like(rest[0])


def _in_proj(x, w, tm, tn, zero_rows=0):
    m, k = x.shape
    _, n = w.shape
    assert Q_COL0 % tn == 0 and ATTN_WIDTH == tn
    steps = (n // tn) * (m // tm)
    in_specs = [pl.BlockSpec((tm, k), lambda j, i: (i, 0)),
                pl.BlockSpec((k, tn), lambda j, i: (0, j))]
    out_specs = [pl.BlockSpec((tm, tn), lambda j, i: (i, j))]
    out_shape = [jax.ShapeDtypeStruct((m, n), BF16)]
    if zero_rows:
        nblk = max(d for d in range(1, steps + 1) if zero_rows % d == 0)
        n_i = m // tm
        out_specs.append(pl.BlockSpec(_tok_shape(zero_rows // nblk),
                                      lambda j, i: (0, jnp.minimum(j * n_i + i, nblk - 1), 0)))
        out_shape.append(jax.ShapeDtypeStruct(_tok_shape(zero_rows), F32))
    out = pl.pallas_call(
        functools.partial(_in_proj_kernel, q_tile=Q_COL0 // tn, zero_fill=bool(zero_rows)),
        grid=(n // tn, m // tm),
        in_specs=in_specs,
        out_specs=out_specs,
        out_shape=out_shape,
        scratch_shapes=[pltpu.VMEM((k, tn), BF16)],
        compiler_params=_cparams(("arbitrary", "arbitrary")),
        name="in_proj",
    )(x, w)
    return out if zero_rows else out[0]


def _attn_kernel(q_ref, k_ref, v_ref, km_ref, vm_ref, u_ref, um_ref, wg_ref, wu_ref, wd_ref,
                 o_ref, wgo_ref, wuo_ref, wdo_ref, vt_ref, vmt_ref, acc_ref, r_ref,
                 *, seq, tq, tk, fc):
    for c in range(D_FF // fc):
        wgo_ref[0, c] = wg_ref[:, c * fc:(c + 1) * fc].astype(BF16)
        wuo_ref[0, c] = wu_ref[:, c * fc:(c + 1) * fc].astype(BF16)
    wdo_ref[...] = wd_ref[...].astype(BF16)

    p = pl.program_id(2)
    n_q = seq // tq
    n_sub = tq // tk

    @pl.when(p == 0)
    def _():
        for j in range(seq // tk):
            vt_ref[j] = v_ref[0, j * tk:(j + 1) * tk, :].astype(F32).T.astype(BF16)
        vmt_ref[...] = vm_ref[...].astype(F32).T.astype(BF16)

    u = u_ref[...]
    krow = lax.broadcasted_iota(I32, (tk, tq), 0)
    qcol = lax.broadcasted_iota(I32, (tk, tq), 1)
    valid_meta = lax.broadcasted_iota(I32, (META_ROWS, tq), 0) >= META_ROWS - N_META

    def q_tile(qi):
        q0 = pl.multiple_of(qi * tq, tq)
        q = q_ref[0, pl.ds(q0, tq), :]
        acc_ref[...] = jnp.zeros_like(acc_ref)
        r_ref[...] = jnp.zeros_like(r_ref)

        def sweep(tiles):
            z2s = [lax.dot_general(k, q, (((1,), (1,)), ((), ())), preferred_element_type=F32)
                   for k, _, _, _ in tiles]
            cums = []
            for z2, (_, _, ut, mask) in zip(z2s, tiles):
                e = jnp.exp2(-jnp.abs(z2))
                s2 = jnp.maximum(z2, 0.0) + jnp.log2(1.0 + e)
                if mask is not None:
                    s2 = jnp.where(mask, s2, 0.0)
                cums.append(jnp.dot(ut, s2.astype(BF16), preferred_element_type=F32))
            r = r_ref[...]
            pv = None
            for z2, cum, (_, vt, _, mask) in zip(z2s, cums, tiles):
                w = jnp.exp2(z2 - cum - r)
                if mask is not None:
                    w = jnp.where(mask, w, 0.0)
                part = jnp.dot(vt, w.astype(BF16), preferred_element_type=F32)
                pv = part if pv is None else pv + part
                r = r + cum[0:1, :]
            acc_ref[...] += pv
            r_ref[...] = r

        def key_tile(j, mask):
            k0 = pl.multiple_of(j * tk, tk)
            return (k_ref[0, pl.ds(k0, tk), :], vt_ref[j], u, mask)

        sweep([key_tile(qi * n_sub + d, krow + d * tk < qcol) for d in reversed(range(n_sub))])

        def full(jj, c):
            base = (qi - 2 - 2 * jj) * n_sub
            sweep([key_tile(base + d, None) for d in reversed(range(2 * n_sub))])
            return c

        lax.fori_loop(0, qi // 2, full, 0)

        @pl.when(qi % 2 == 1)
        def _():
            sweep([key_tile(d, None) for d in reversed(range(n_sub))])

        sweep([(km_ref[...], vmt_ref[...], um_ref[...], valid_meta)])
        o_ref[0, pl.ds(q0, tq), :] = acc_ref[...].T.astype(o_ref.dtype)

    q_tile(p)
    q_tile(n_q - 1 - p)


def _tri_upper_incl(n):
    r = lax.broadcasted_iota(I32, (n, n), 0)
    c = lax.broadcasted_iota(I32, (n, n), 1)
    return (c >= r).astype(BF16)


def _attention(proj, proj_meta, w_gate, w_up, w_down, bsz, seq, tq, tk, fc):
    proj3 = proj.reshape(bsz, seq, IN_WIDTH)
    qb = Q_COL0 // HEAD_DIM
    kb, vb = qb + N_HEADS, qb + 2 * N_HEADS
    n_pair = seq // tq // 2
    n_exp, d, dff = w_gate.shape
    n_fc = dff // fc
    steps = bsz * N_HEADS * n_pair
    rc = (n_exp * d) // steps
    assert dff == d and rc * steps == n_exp * d and d % rc == 0 and rc % (2 * SUBLANES) == 0
    blk_per_e = d // rc

    def step(b, h, p):
        return (b * N_HEADS + h) * n_pair + p

    seq_spec = lambda col0: pl.BlockSpec((1, seq, HEAD_DIM), lambda b, h, p: (b, 0, col0 + h))
    w_in_spec = pl.BlockSpec((rc, dff), lambda b, h, p: (step(b, h, p), 0))
    wgu_out_spec = pl.BlockSpec((1, n_fc, rc, fc),
                                lambda b, h, p: (step(b, h, p) // blk_per_e, 0, step(b, h, p) % blk_per_e, 0))
    out, wg_bf, wu_bf, wd_bf = pl.pallas_call(
        functools.partial(_attn_kernel, seq=seq, tq=tq, tk=tk, fc=fc),
        grid=(bsz, N_HEADS, n_pair),
        in_specs=[seq_spec(qb), seq_spec(kb), seq_spec(vb),
                  pl.BlockSpec((META_ROWS, HEAD_DIM), lambda b, h, p: (0, kb + h)),
                  pl.BlockSpec((META_ROWS, HEAD_DIM), lambda b, h, p: (0, vb + h)),
                  pl.BlockSpec((tk, tk), lambda b, h, p: (0, 0)),
                  pl.BlockSpec((META_ROWS, META_ROWS), lambda b, h, p: (0, 0)),
                  w_in_spec, w_in_spec,
                  pl.BlockSpec((rc, d), lambda b, h, p: (step(b, h, p), 0))],
        out_specs=[pl.BlockSpec((1, seq, HEAD_DIM), lambda b, h, p: (b, 0, h)),
                   wgu_out_spec, wgu_out_spec,
                   pl.BlockSpec((rc, d), lambda b, h, p: (step(b, h, p), 0))],
        out_shape=[jax.ShapeDtypeStruct((bsz, seq, ATTN_WIDTH), BF16),
                   jax.ShapeDtypeStruct((n_exp, n_fc, d, fc), BF16),
                   jax.ShapeDtypeStruct((n_exp, n_fc, d, fc), BF16),
                   jax.ShapeDtypeStruct((n_exp * dff, d), BF16)],
        scratch_shapes=[pltpu.VMEM((seq // tk, HEAD_DIM, tk), BF16),
                        pltpu.VMEM((HEAD_DIM, META_ROWS), BF16),
                        pltpu.VMEM((HEAD_DIM, tq), F32),
                        pltpu.VMEM((1, tq), F32)],
        compiler_params=_cparams(("arbitrary", "arbitrary", "arbitrary")),
        name="stickbreak_attn",
    )(proj3, proj3, proj3, proj_meta, proj_meta, _tri_upper_incl(tk), _tri_upper_incl(META_ROWS),
      w_gate.reshape(n_exp * d, dff), w_up.reshape(n_exp * d, dff), w_down.reshape(n_exp * dff, d))
    return out.reshape(bsz * seq, ATTN_WIDTH), wg_bf, wu_bf, wd_bf.reshape(n_exp, dff, d)


def _slab_idx(first_tok, n_tok, c):
    h, s = divmod(c, TOK_ROWS)
    return (h, pl.ds(first_tok * TOK_ROWS + s, n_tok, stride=TOK_ROWS), slice(None))


N_CHUNKS = D_MODEL // LANES


def _store_slabs(vals, out_ref, n_tok):
    for c in range(N_CHUNKS):
        out_ref[_slab_idx(0, n_tok, c)] = vals[:, c * LANES:(c + 1) * LANES]


def _mixer_kernel(u_ref, bp_ref, cp_ref, gc_ref, ga_ref, o_ref, x_ref, um_ref, cm_ref,
                  convw_ref, wc_ref, wa_ref, wo_ref, gffn_ref, wrh_ref, wrl_ref, br_ref, ltri_ref,
                  h1_ref, npk_ref, meta_ref, wts_ref, cnt_ref,
                  cu_ref, carry_ref, *, tm, tiles_per_seq):
    i = pl.program_id(0)
    first = (i % tiles_per_seq) == 0

    @pl.when(i == 0)
    def _():
        carry_ref[...] = jnp.zeros_like(carry_ref)

    @pl.when(first)
    def _():
        cum = cm_ref[...].astype(F32) * um_ref[...].astype(F32)
        cu_ref[0:SUBLANES, :] = cum[SUBLANES:2 * SUBLANES, :]

    @pl.when(jnp.logical_not(first))
    def _():
        cu_ref[0:SUBLANES, :] = cu_ref[tm:tm + SUBLANES, :]

    cu = cp_ref[...].astype(F32) * u_ref[...].astype(F32)
    cu_ref[SUBLANES:tm + SUBLANES, :] = cu
    cw = convw_ref[...]
    conv = (cu_ref[SUBLANES - 2:tm + SUBLANES - 2, :] * cw[0:1, :]
            + cu_ref[SUBLANES - 1:tm + SUBLANES - 1, :] * cw[1:2, :]
            + cu * cw[2:3, :])
    y_conv = jnp.dot((bp_ref[...].astype(F32) * conv).astype(BF16), wc_ref[...],
                     preferred_element_type=F32)
    y_attn = jnp.dot(o_ref[...], wa_ref[...], preferred_element_type=F32)
    merged = (jax.nn.sigmoid(gc_ref[...].astype(F32)) * y_conv
              + jax.nn.sigmoid(ga_ref[...].astype(F32)) * y_attn)
    h1 = x_ref[...] + jnp.dot(merged.astype(BF16), wo_ref[...], preferred_element_type=F32)
    h1_ref[...] = h1

    ms = jnp.mean(h1 * h1, axis=-1, keepdims=True)
    n = h1 * lax.rsqrt(ms + RMS_EPS) * gffn_ref[...]
    _store_slabs(n, npk_ref, tm)

    n_hi = n.astype(BF16)
    n_lo = (n - n_hi.astype(F32)).astype(BF16)
    logits = (jnp.dot(n_hi, wrh_ref[...], preferred_element_type=F32)
              + jnp.dot(n_lo, wrh_ref[...], preferred_element_type=F32)
              + jnp.dot(n_hi, wrl_ref[...], preferred_element_type=F32)) + br_ref[...]
    lane = lax.broadcasted_iota(I32, (tm, LANES), 1)
    lg = jnp.where(lane < N_EXPERTS, logits, -jnp.inf)

    sels, tops, idxs = [], [], []
    for _ in range(TOP_K):
        m = jnp.max(lg, axis=-1, keepdims=True)
        idx = jnp.min(jnp.where(lg == m, lane, LANES), axis=-1, keepdims=True)
        sel = lane == idx
        sels.append(sel)
        tops.append(m)
        idxs.append(idx)
        lg = jnp.where(sel, -jnp.inf, lg)
    exps = [jnp.exp(t - tops[0]) for t in tops]
    denom = exps[0] + exps[1] + exps[2] + exps[3]
    wts = [e / denom for e in exps]

    onehot = jnp.zeros((tm, LANES), F32)
    for sel in sels:
        onehot = onehot + sel.astype(F32)
    base = carry_ref[0:1, :] + jnp.dot(ltri_ref[...], onehot.astype(BF16), preferred_element_type=F32)
    meta = jnp.zeros((tm, LANES), I32)
    wlanes = jnp.zeros((tm, LANES), F32)
    for k in range(TOP_K):
        rank = jnp.sum(jnp.where(sels[k], base, 0.0), axis=-1, keepdims=True)
        meta = jnp.where(lane == k, idxs[k], meta)
        meta = jnp.where(lane == TOP_K + k, rank.astype(I32), meta)
        wlanes = jnp.where(lane == k, wts[k], wlanes)
    meta_ref[...] = meta
    wts_ref[...] = wlanes
    carry_ref[0:1, :] = carry_ref[0:1, :] + jnp.sum(onehot, axis=0, keepdims=True)
    cnt_ref[...] = jnp.broadcast_to(carry_ref[0:1, :], cnt_ref.shape).astype(I32)


def _tri_strict_lower(n):
    r = lax.broadcasted_iota(I32, (n, n), 0)
    c = lax.broadcasted_iota(I32, (n, n), 1)
    return (c < r).astype(BF16)


def _const_spec(shape):
    return pl.BlockSpec(shape, lambda i: (0,) * len(shape))


def _mixer(proj, proj_meta, attn_o, x2d, conv_w, wc, wa, wo, g_ffn, wr_hi, wr_lo, b_r, seq, tm):
    n_tok = x2d.shape[0]
    tiles_per_seq = seq // tm
    meta_blk = META_ROWS // (2 * SUBLANES) - 1
    in_specs = [
        pl.BlockSpec((tm, CONV_CH), lambda i: (i, 0)),
        pl.BlockSpec((tm, CONV_CH), lambda i: (i, 1)),
        pl.BlockSpec((tm, CONV_CH), lambda i: (i, 2)),
        pl.BlockSpec((tm, D_MODEL), lambda i: (i, 3)),
        pl.BlockSpec((tm, D_MODEL), lambda i: (i, 4)),
        pl.BlockSpec((tm, ATTN_WIDTH), lambda i: (i, 0)),
        pl.BlockSpec((tm, D_MODEL), lambda i: (i, 0)),
        pl.BlockSpec((2 * SUBLANES, CONV_CH), lambda i: (meta_blk, 0)),
        pl.BlockSpec((2 * SUBLANES, CONV_CH), lambda i: (meta_blk, 2)),
        _const_spec((SUBLANES, CONV_CH)),
        _const_spec((CONV_CH, D_MODEL)),
        _const_spec((ATTN_WIDTH, D_MODEL)),
        _const_spec((D_MODEL, D_MODEL)),
        _const_spec((1, D_MODEL)),
        _const_spec((D_MODEL, LANES)),
        _const_spec((D_MODEL, LANES)),
        _const_spec((1, LANES)),
        _const_spec((tm, tm)),
    ]
    out_specs = [
        pl.BlockSpec((tm, D_MODEL), lambda i: (i, 0)),
        pl.BlockSpec(_tok_shape(tm), lambda i: (0, i, 0)),
        pl.BlockSpec((tm, LANES), lambda i: (i, 0)),
        pl.BlockSpec((tm, LANES), lambda i: (i, 0)),
        _const_spec((SUBLANES, LANES)),
    ]
    out_shape = [
        jax.ShapeDtypeStruct((n_tok, D_MODEL), F32),
        jax.ShapeDtypeStruct(_tok_shape(n_tok), F32),
        jax.ShapeDtypeStruct((n_tok, LANES), I32),
        jax.ShapeDtypeStruct((n_tok, LANES), F32),
        jax.ShapeDtypeStruct((SUBLANES, LANES), I32),
    ]
    conv_w8 = jnp.pad(conv_w, ((0, SUBLANES - CONV_K), (0, 0)))
    return pl.pallas_call(
        functools.partial(_mixer_kernel, tm=tm, tiles_per_seq=tiles_per_seq),
        grid=(n_tok // tm,),
        in_specs=in_specs,
        out_specs=out_specs,
        out_shape=out_shape,
        scratch_shapes=[pltpu.VMEM((tm + SUBLANES, CONV_CH), F32), pltpu.VMEM((SUBLANES, LANES), F32)],
        compiler_params=_cparams(("arbitrary",)),
        name="mixer_out",
    )(proj, proj, proj, proj, proj, attn_o, x2d, proj_meta, proj_meta, conv_w8, wc, wa, wo,
      g_ffn.reshape(1, D_MODEL), wr_hi, wr_lo, b_r, _tri_strict_lower(tm))


def _row_slab(ref, row):
    return ref.at[:, pl.ds(pl.multiple_of(row * TOK_ROWS, TOK_ROWS), TOK_ROWS), :]


def _wait_rows(hbm_ref, vmem_or_hbm_ref, sem, n_rows):
    n_sub = n_rows * TOK_ROWS
    pltpu.make_async_copy(hbm_ref.at[:, pl.ds(0, n_sub), :], vmem_or_hbm_ref.at[:, pl.ds(0, n_sub), :],
                          sem).wait()


def _dispatch_kernel(eid_ref, rank_ref, offs_ref, npk_ref, xg_in_ref, xg_ref, sem, *, tm):
    del xg_in_ref

    def issue(t, c):
        for k in range(TOP_K):
            j = t * TOP_K + k
            dst = offs_ref[eid_ref[j]] + rank_ref[j]
            pltpu.make_async_copy(_row_slab(npk_ref, t), _row_slab(xg_ref, dst), sem).start()
        return c

    lax.fori_loop(0, tm, issue, 0, unroll=4)
    _wait_rows(xg_ref, xg_ref, sem, tm * TOP_K)


def _dispatch(eid_flat, rank_flat, offs, npk, xg0, tm):
    n_tok = npk.shape[1] // TOK_ROWS
    n_rows = xg0.shape[1] // TOK_ROWS
    smem_blk = pl.BlockSpec((tm * TOP_K,), lambda i: (i,), memory_space=pltpu.SMEM)
    return pl.pallas_call(
        functools.partial(_dispatch_kernel, tm=tm),
        grid=(n_tok // tm,),
        in_specs=[smem_blk, smem_blk,
                  pl.BlockSpec(memory_space=pltpu.SMEM),
                  pl.BlockSpec(_tok_shape(tm), lambda i: (0, i, 0)),
                  pl.BlockSpec(memory_space=pl.ANY)],
        out_specs=pl.BlockSpec(memory_space=pl.ANY),
        out_shape=jax.ShapeDtypeStruct(_tok_shape(n_rows), F32),
        scratch_shapes=[pltpu.SemaphoreType.DMA(())],
        input_output_aliases={4: 0},
        compiler_params=_cparams(("arbitrary",)),
        name="moe_dispatch",
    )(eid_flat, rank_flat, offs, npk, xg0)


def _expert_kernel(te_ref, tb_ref, nu_ref, x_ref, wg_ref, bg_ref, wu_ref, bu_ref, wd_ref, bd_ref,
                   o_ref, xs_ref, acc_ref, *, n_fc, tr):
    del te_ref, tb_ref
    t = pl.program_id(0)
    f = pl.program_id(1)

    @pl.when(t < nu_ref[0])
    def _():
        @pl.when(f == 0)
        def _():
            for c in range(N_CHUNKS):
                xs_ref[:, c * LANES:(c + 1) * LANES] = x_ref[_slab_idx(0, tr, c)].astype(BF16)
            acc_ref[...] = jnp.zeros_like(acc_ref)

        x = xs_ref[...]
        gate = jnp.dot(x, wg_ref[0, 0], preferred_element_type=F32) + bg_ref[0]
        up = jnp.dot(x, wu_ref[0, 0], preferred_element_type=F32) + bu_ref[0]
        gate = jnp.minimum(gate, SWIGLU_LIMIT)
        up = jnp.clip(up, -SWIGLU_LIMIT, SWIGLU_LIMIT)
        act = (up + 1.0) * (gate * jax.nn.sigmoid(SWIGLU_ALPHA * gate))
        acc_ref[...] += jnp.dot(act.astype(BF16), wd_ref[0], preferred_element_type=F32)

        @pl.when(f == n_fc - 1)
        def _():
            _store_slabs(acc_ref[...] + bd_ref[0], o_ref, tr)


def _experts(tile_expert, tile_block, n_used, xg, w_gate, b_gate, w_up, b_up, w_down, b_down, tr):
    n_rows = xg.shape[1] // TOK_ROWS
    n_exp, n_fc, d, fc = w_gate.shape
    dff = n_fc * fc
    grid_spec = pltpu.PrefetchScalarGridSpec(
        num_scalar_prefetch=3,
        grid=(n_rows // tr, n_fc),
        in_specs=[
            pl.BlockSpec(_tok_shape(tr), lambda t, f, te, tb, nu: (0, tb[t], 0)),
            pl.BlockSpec((1, 1, d, fc), lambda t, f, te, tb, nu: (te[t], f, 0, 0)),
            pl.BlockSpec((1, 1, fc), lambda t, f, te, tb, nu: (te[t], 0, f)),
            pl.BlockSpec((1, 1, d, fc), lambda t, f, te, tb, nu: (te[t], f, 0, 0)),
            pl.BlockSpec((1, 1, fc), lambda t, f, te, tb, nu: (te[t], 0, f)),
            pl.BlockSpec((1, fc, d), lambda t, f, te, tb, nu: (te[t], f, 0)),
            pl.BlockSpec((1, 1, d), lambda t, f, te, tb, nu: (te[t], 0, 0)),
        ],
        out_specs=pl.BlockSpec(_tok_shape(tr), lambda t, f, te, tb, nu: (0, tb[t], 0)),
        scratch_shapes=[pltpu.VMEM((tr, d), BF16), pltpu.VMEM((tr, d), F32)],
    )
    return pl.pallas_call(
        functools.partial(_expert_kernel, n_fc=n_fc, tr=tr),
        grid_spec=grid_spec,
        out_shape=jax.ShapeDtypeStruct(_tok_shape(n_rows), F32),
        input_output_aliases={3: 0},
        compiler_params=_cparams(("arbitrary", "arbitrary")),
        name="moe_experts",
    )(tile_expert, tile_block, n_used, xg, w_gate, b_gate.reshape(n_exp, 1, dff), w_up,
      b_up.reshape(n_exp, 1, dff), w_down, b_down.reshape(n_exp, 1, d))


def _combine_kernel(eid_ref, rank_ref, offs_ref, og_ref, wts_ref, h1_ref, gfin_ref, out_ref,
                    gbuf_ref, h2_ref, sem, *, tm):
    def issue(t, c):
        for k in range(TOP_K):
            j = t * TOP_K + k
            src = offs_ref[eid_ref[j]] + rank_ref[j]
            pltpu.make_async_copy(_row_slab(og_ref, src), _row_slab(gbuf_ref, k * tm + t), sem).start()
        return c

    lax.fori_loop(0, tm, issue, 0, unroll=4)
    _wait_rows(og_ref, gbuf_ref, sem, tm * TOP_K)

    wts = wts_ref[...]
    wk = [wts[:, k:k + 1] for k in range(TOP_K)]
    for c in range(N_CHUNKS):
        y = h1_ref[:, c * LANES:(c + 1) * LANES]
        for k in range(TOP_K):
            y = y + wk[k] * gbuf_ref[_slab_idx(k * tm, tm, c)]
        h2_ref[:, c * LANES:(c + 1) * LANES] = y
    h2 = h2_ref[...]
    ms = jnp.mean(h2 * h2, axis=-1, keepdims=True)
    out_ref[...] = h2 * lax.rsqrt(ms + RMS_EPS) * gfin_ref[...]


def _combine(eid_flat, rank_flat, offs, og, wts, h1, g_final, tm):
    n_tok = h1.shape[0]
    smem_blk = pl.BlockSpec((tm * TOP_K,), lambda i: (i,), memory_space=pltpu.SMEM)
    return pl.pallas_call(
        functools.partial(_combine_kernel, tm=tm),
        grid=(n_tok // tm,),
        in_specs=[smem_blk, smem_blk,
                  pl.BlockSpec(memory_space=pltpu.SMEM),
                  pl.BlockSpec(memory_space=pl.ANY),
                  pl.BlockSpec((tm, LANES), lambda i: (i, 0)),
                  pl.BlockSpec((tm, D_MODEL), lambda i: (i, 0)),
                  pl.BlockSpec((1, D_MODEL), lambda i: (0, 0))],
        out_specs=pl.BlockSpec((tm, D_MODEL), lambda i: (i, 0)),
        out_shape=jax.ShapeDtypeStruct((n_tok, D_MODEL), F32),
        scratch_shapes=[pltpu.VMEM(_tok_shape(tm * TOP_K), F32),
                        pltpu.VMEM((tm, D_MODEL), F32),
                        pltpu.SemaphoreType.DMA(())],
        compiler_params=_cparams(("arbitrary",)),
        name="moe_combine",
    )(eid_flat, rank_flat, offs, og, wts, h1, g_final.reshape(1, D_MODEL))


def _routing_tables(counts, tr, n_tiles):
    ntile = (counts + tr - 1) // tr
    tiles_cum = jnp.cumsum(ntile)
    offs = ((tiles_cum - ntile) * tr).astype(I32)
    n_used = tiles_cum[-1]
    t = jnp.minimum(jnp.arange(n_tiles, dtype=I32), n_used - 1)
    tile_expert = jnp.sum((tiles_cum[None, :] <= t[:, None]).astype(I32), axis=1)
    return offs, tile_expert, t.astype(I32), n_used.reshape(1).astype(I32)


def kernel(x, meta_tokens, g_mix, w_in, conv_w, w_conv_out, w_attn_out, w_o, g_ffn, w_router,
           b_router, w_gate, b_gate, w_up, b_up, w_down, b_down, g_final):
    assert g_mix.shape[0] == 1, "single-layer trunk"
    bsz, seq, d = x.shape
    n_tok = bsz * seq
    x2d = x.reshape(n_tok, d)

    n_tiles = (n_tok * TOP_K) // EXPERT_TR + N_EXPERTS
    hn = _rmsnorm_bf16(x2d, g_mix[0], RMS_TM)
    proj, xg0 = _in_proj(hn, w_in[0], PROJ_TM, PROJ_TN, zero_rows=n_tiles * EXPERT_TR)
    meta_pad = jnp.pad(meta_tokens.astype(x.dtype), ((META_ROWS - N_META, 0), (0, 0)))
    hn_meta = _rmsnorm_bf16(meta_pad, g_mix[0], META_ROWS)
    proj_meta = _in_proj(hn_meta, w_in[0], META_ROWS, PROJ_TN)

    attn_o, wg_bf, wu_bf, wd_bf = _attention(proj, proj_meta, w_gate[0], w_up[0], w_down[0],
                                             bsz, seq, ATT_TQ, ATT_TK, EXPERT_FC)

    wr = jnp.pad(w_router[0], ((0, 0), (0, LANES - N_EXPERTS)))
    wr_hi = wr.astype(BF16)
    wr_lo = (wr - wr_hi.astype(F32)).astype(BF16)
    b_r = jnp.pad(b_router[0], (0, LANES - N_EXPERTS)).reshape(1, LANES)
    h1, npk, meta, wts, cnt = _mixer(proj, proj_meta, attn_o, x2d, conv_w[0],
                                     w_conv_out[0].astype(BF16), w_attn_out[0].astype(BF16),
                                     w_o[0].astype(BF16), g_ffn[0], wr_hi, wr_lo, b_r, seq, MIX_TM)

    offs, tile_expert, tile_block, n_used = _routing_tables(cnt[0, :N_EXPERTS], EXPERT_TR, n_tiles)
    eid_flat = meta[:, 0:TOP_K].reshape(-1)
    rank_flat = meta[:, TOP_K:2 * TOP_K].reshape(-1)

    xg = _dispatch(eid_flat, rank_flat, offs, npk, xg0, DISPATCH_TM)
    og = _experts(tile_expert, tile_block, n_used, xg, wg_bf, b_gate[0], wu_bf, b_up[0],
                  wd_bf, b_down[0], EXPERT_TR)
    out = _combine(eid_flat, rank_flat, offs, og, wts, h1, g_final, COMBINE_TM)
    return out.reshape(bsz, seq, d)
```

```python
import functools
import math

import jax
import jax.numpy as jnp
from jax import lax
from jax.experimental import pallas as pl
from jax.experimental.pallas import tpu as pltpu

F32 = jnp.float32
BF16 = jnp.bfloat16
I32 = jnp.int32

D_MODEL = 2048
N_META = 16
N_HEADS = 8
HEAD_DIM = 128
ATTN_WIDTH = N_HEADS * HEAD_DIM
CONV_CH = D_MODEL // 2
CONV_K = 3
N_EXPERTS = 32
TOP_K = 4
D_FF = D_MODEL
SWIGLU_LIMIT = 7.0
SWIGLU_ALPHA = 1.702
RMS_EPS = 1e-5
IN_WIDTH = 3 * CONV_CH + 3 * ATTN_WIDTH + 2 * D_MODEL

LANES = 128
SUBLANES = 8
META_ROWS = 128
LOG2E = 1.4426950408889634
VMEM_LIMIT = 56 * 1024 * 1024

RMS_TM = 512
PROJ_TM = 1024
PROJ_TN = 1024
ATT_TQ = 512
ATT_TK = 256
MIX_TM = 256
DISPATCH_TM = 256
EXPERT_TR = 768
EXPERT_FC = 512
EXPERT_SUB = 256
COMBINE_TM = 128
TOK_ROWS = SUBLANES
TOK_HALVES = D_MODEL // (TOK_ROWS * LANES)


def _tok_shape(n_rows):
    return (TOK_HALVES, n_rows * TOK_ROWS, LANES)


def _cparams(sem, vmem=VMEM_LIMIT):
    return pltpu.CompilerParams(dimension_semantics=sem, vmem_limit_bytes=vmem)


def _rmsnorm_kernel(x_ref, g_ref, o_ref):
    x = x_ref[...].astype(F32)
    ms = jnp.mean(x * x, axis=-1, keepdims=True)
    o_ref[...] = (x * lax.rsqrt(ms + RMS_EPS) * g_ref[...]).astype(o_ref.dtype)


def _rmsnorm_bf16(x, g, tm):
    m, d = x.shape
    return pl.pallas_call(
        _rmsnorm_kernel,
        grid=(m // tm,),
        in_specs=[pl.BlockSpec((tm, d), lambda i: (i, 0)),
                  pl.BlockSpec((1, d), lambda i: (0, 0))],
        out_specs=pl.BlockSpec((tm, d), lambda i: (i, 0)),
        out_shape=jax.ShapeDtypeStruct((m, d), BF16),
        compiler_params=_cparams(("arbitrary",)),
        name="rmsnorm",
    )(x, g.reshape(1, d))


ZSCALE = LOG2E / math.sqrt(HEAD_DIM)
Q_COL0 = 3 * CONV_CH


def _in_proj_kernel(x_ref, w_ref, o_ref, *rest, q_tile, zero_fill):
    wbf_ref = rest[-1]
    j = pl.program_id(0)

    @pl.when(pl.program_id(1) == 0)
    def _():
        wbf_ref[...] = w_ref[...].astype(BF16)

    acc = jnp.dot(x_ref[...], wbf_ref[...], preferred_element_type=F32)
    o_ref[...] = (acc * jnp.where(j == q_tile, ZSCALE, 1.0)).astype(o_ref.dtype)
    if zero_fill:
        rest[0][...] = jnp.zeros_like(rest[0])


def _in_proj(x, w, tm, tn, zero_rows=0):
    m, k = x.shape
    _, n = w.shape
    assert Q_COL0 % tn == 0 and ATTN_WIDTH == tn
    steps = (n // tn) * (m // tm)
    in_specs = [pl.BlockSpec((tm, k), lambda j, i: (i, 0)),
                pl.BlockSpec((k, tn), lambda j, i: (0, j))]
    out_specs = [pl.BlockSpec((tm, tn), lambda j, i: (i, j))]
    out_shape = [jax.ShapeDtypeStruct((m, n), BF16)]
    if zero_rows:
        nblk = max(d for d in range(1, steps + 1) if zero_rows % d == 0)
        n_i = m // tm
        out_specs.append(pl.BlockSpec(_tok_shape(zero_rows // nblk),
                                      lambda j, i: (0, jnp.minimum(j * n_i + i, nblk - 1), 0)))
        out_shape.append(jax.ShapeDtypeStruct(_tok_shape(zero_rows), F32))
    out = pl.pallas_call(
        functools.partial(_in_proj_kernel, q_tile=Q_COL0 // tn, zero_fill=bool(zero_rows)),
        grid=(n // tn, m // tm),
        in_specs=in_specs,
        out_specs=out_specs,
        out_shape=out_shape,
        scratch_shapes=[pltpu.VMEM((k, tn), BF16)],
        compiler_params=_cparams(("arbitrary", "arbitrary")),
        name="in_proj",
    )(x, w)
    return out if zero_rows else out[0]


def _attn_kernel(q_ref, k_ref, v_ref, km_ref, vm_ref, u_ref, um_ref, wg_ref, wu_ref, wd_ref,
                 o_ref, wgo_ref, wuo_ref, wdo_ref, vt_ref, vmt_ref, acc_ref, r_ref,
                 *, seq, tq, tk, fc):
    for c in range(D_FF // fc):
        wgo_ref[0, c] = wg_ref[:, c * fc:(c + 1) * fc].astype(BF16)
        wuo_ref[0, c] = wu_ref[:, c * fc:(c + 1) * fc].astype(BF16)
    wdo_ref[...] = wd_ref[...].astype(BF16)

    p = pl.program_id(2)
    n_q = seq // tq
    n_sub = tq // tk

    @pl.when(p == 0)
    def _():
        for j in range(seq // tk):
            vt_ref[j] = v_ref[0, j * tk:(j + 1) * tk, :].astype(F32).T.astype(BF16)
        vmt_ref[...] = vm_ref[...].astype(F32).T.astype(BF16)

    u = u_ref[...]
    krow = lax.broadcasted_iota(I32, (tk, tq), 0)
    qcol = lax.broadcasted_iota(I32, (tk, tq), 1)
    valid_meta = lax.broadcasted_iota(I32, (META_ROWS, tq), 0) >= META_ROWS - N_META

    def q_tile(qi):
        q0 = pl.multiple_of(qi * tq, tq)
        q = q_ref[0, pl.ds(q0, tq), :]
        acc_ref[...] = jnp.zeros_like(acc_ref)
        r_ref[...] = jnp.zeros_like(r_ref)

        def sweep(tiles):
            z2s = [lax.dot_general(k, q, (((1,), (1,)), ((), ())), preferred_element_type=F32)
                   for k, _, _, _ in tiles]
            cums = []
            for z2, (_, _, ut, mask) in zip(z2s, tiles):
                e = jnp.exp2(-jnp.abs(z2))
                s2 = jnp.maximum(z2, 0.0) + jnp.log2(1.0 + e)
                if mask is not None:
                    s2 = jnp.where(mask, s2, 0.0)
                cums.append(jnp.dot(ut, s2.astype(BF16), preferred_element_type=F32))
            r = r_ref[...]
            pv = None
            for z2, cum, (_, vt, _, mask) in zip(z2s, cums, tiles):
                w = jnp.exp2(z2 - cum - r)
                if mask is not None:
                    w = jnp.where(mask, w, 0.0)
                part = jnp.dot(vt, w.astype(BF16), preferred_element_type=F32)
                pv = part if pv is None else pv + part
                r = r + cum[0:1, :]
            acc_ref[...] += pv
            r_ref[...] = r

        def key_tile(j, mask):
            k0 = pl.multiple_of(j * tk, tk)
            return (k_ref[0, pl.ds(k0, tk), :], vt_ref[j], u, mask)

        sweep([key_tile(qi * n_sub + d, krow + d * tk < qcol) for d in reversed(range(n_sub))])

        def full(jj, c):
            base = (qi - 2 - 2 * jj) * n_sub
            sweep([key_tile(base + d, None) for d in reversed(range(2 * n_sub))])
            return c

        lax.fori_loop(0, qi // 2, full, 0)

        @pl.when(qi % 2 == 1)
        def _():
            sweep([key_tile(d, None) for d in reversed(range(n_sub))])

        sweep([(km_ref[...], vmt_ref[...], um_ref[...], valid_meta)])
        o_ref[0, pl.ds(q0, tq), :] = acc_ref[...].T.astype(o_ref.dtype)

    q_tile(p)
    q_tile(n_q - 1 - p)


def _tri_upper_incl(n):
    r = lax.broadcasted_iota(I32, (n, n), 0)
    c = lax.broadcasted_iota(I32, (n, n), 1)
    return (c >= r).astype(BF16)


def _attention(proj, proj_meta, w_gate, w_up, w_down, bsz, seq, tq, tk, fc):
    proj3 = proj.reshape(bsz, seq, IN_WIDTH)
    qb = Q_COL0 // HEAD_DIM
    kb, vb = qb + N_HEADS, qb + 2 * N_HEADS
    n_pair = seq // tq // 2
    n_exp, d, dff = w_gate.shape
    n_fc = dff // fc
    steps = bsz * N_HEADS * n_pair
    rc = (n_exp * d) // steps
    assert dff == d and rc * steps == n_exp * d and d % rc == 0 and rc % (2 * SUBLANES) == 0
    blk_per_e = d // rc

    def step(b, h, p):
        return (b * N_HEADS + h) * n_pair + p

    seq_spec = lambda col0: pl.BlockSpec((1, seq, HEAD_DIM), lambda b, h, p: (b, 0, col0 + h))
    w_in_spec = pl.BlockSpec((rc, dff), lambda b, h, p: (step(b, h, p), 0))
    wgu_out_spec = pl.BlockSpec((1, n_fc, rc, fc),
                                lambda b, h, p: (step(b, h, p) // blk_per_e, 0, step(b, h, p) % blk_per_e, 0))
    out, wg_bf, wu_bf, wd_bf = pl.pallas_call(
        functools.partial(_attn_kernel, seq=seq, tq=tq, tk=tk, fc=fc),
        grid=(bsz, N_HEADS, n_pair),
        in_specs=[seq_spec(qb), seq_spec(kb), seq_spec(vb),
                  pl.BlockSpec((META_ROWS, HEAD_DIM), lambda b, h, p: (0, kb + h)),
                  pl.BlockSpec((META_ROWS, HEAD_DIM), lambda b, h, p: (0, vb + h)),
                  pl.BlockSpec((tk, tk), lambda b, h, p: (0, 0)),
                  pl.BlockSpec((META_ROWS, META_ROWS), lambda b, h, p: (0, 0)),
                  w_in_spec, w_in_spec,
                  pl.BlockSpec((rc, d), lambda b, h, p: (step(b, h, p), 0))],
        out_specs=[pl.BlockSpec((1, seq, HEAD_DIM), lambda b, h, p: (b, 0, h)),
                   wgu_out_spec, wgu_out_spec,
                   pl.BlockSpec((rc, d), lambda b, h, p: (step(b, h, p), 0))],
        out_shape=[jax.ShapeDtypeStruct((bsz, seq, ATTN_WIDTH), BF16),
                   jax.ShapeDtypeStruct((n_exp, n_fc, d, fc), BF16),
                   jax.ShapeDtypeStruct((n_exp, n_fc, d, fc), BF16),
                   jax.ShapeDtypeStruct((n_exp * dff, d), BF16)],
        scratch_shapes=[pltpu.VMEM((seq // tk, HEAD_DIM, tk), BF16),
                        pltpu.VMEM((HEAD_DIM, META_ROWS), BF16),
                        pltpu.VMEM((HEAD_DIM, tq), F32),
                        pltpu.VMEM((1, tq), F32)],
        compiler_params=_cparams(("arbitrary", "arbitrary", "arbitrary")),
        name="stickbreak_attn",
    )(proj3, proj3, proj3, proj_meta, proj_meta, _tri_upper_incl(tk), _tri_upper_incl(META_ROWS),
      w_gate.reshape(n_exp * d, dff), w_up.reshape(n_exp * d, dff), w_down.reshape(n_exp * dff, d))
    return out.reshape(bsz * seq, ATTN_WIDTH), wg_bf, wu_bf, wd_bf.reshape(n_exp, dff, d)


def _slab_idx(first_tok, n_tok, c):
    h, s = divmod(c, TOK_ROWS)
    return (h, pl.ds(first_tok * TOK_ROWS + s, n_tok, stride=TOK_ROWS), slice(None))


N_CHUNKS = D_MODEL // LANES


def _store_slabs(vals, out_ref, n_tok):
    for c in range(N_CHUNKS):
        out_ref[_slab_idx(0, n_tok, c)] = vals[:, c * LANES:(c + 1) * LANES]


def _mixer_kernel(u_ref, bp_ref, cp_ref, gc_ref, ga_ref, o_ref, x_ref, um_ref, cm_ref,
                  convw_ref, wc_ref, wa_ref, wo_ref, gffn_ref, wrh_ref, wrl_ref, br_ref, ltri_ref,
                  h1_ref, npk_ref, meta_ref, wts_ref, cnt_ref,
                  cu_ref, carry_ref, *, tm, tiles_per_seq):
    i = pl.program_id(0)
    first = (i % tiles_per_seq) == 0

    @pl.when(i == 0)
    def _():
        carry_ref[...] = jnp.zeros_like(carry_ref)

    @pl.when(first)
    def _():
        cum = cm_ref[...].astype(F32) * um_ref[...].astype(F32)
        cu_ref[0:SUBLANES, :] = cum[SUBLANES:2 * SUBLANES, :]

    @pl.when(jnp.logical_not(first))
    def _():
        cu_ref[0:SUBLANES, :] = cu_ref[tm:tm + SUBLANES, :]

    cu = cp_ref[...].astype(F32) * u_ref[...].astype(F32)
    cu_ref[SUBLANES:tm + SUBLANES, :] = cu
    cw = convw_ref[...]
    conv = (cu_ref[SUBLANES - 2:tm + SUBLANES - 2, :] * cw[0:1, :]
            + cu_ref[SUBLANES - 1:tm + SUBLANES - 1, :] * cw[1:2, :]
            + cu * cw[2:3, :])
    y_conv = jnp.dot((bp_ref[...].astype(F32) * conv).astype(BF16), wc_ref[...],
                     preferred_element_type=F32)
    y_attn = jnp.dot(o_ref[...], wa_ref[...], preferred_element_type=F32)
    merged = (jax.nn.sigmoid(gc_ref[...].astype(F32)) * y_conv
              + jax.nn.sigmoid(ga_ref[...].astype(F32)) * y_attn)
    h1 = x_ref[...] + jnp.dot(merged.astype(BF16), wo_ref[...], preferred_element_type=F32)
    h1_ref[...] = h1

    ms = jnp.mean(h1 * h1, axis=-1, keepdims=True)
    n = h1 * lax.rsqrt(ms + RMS_EPS) * gffn_ref[...]
    _store_slabs(n, npk_ref, tm)

    n_hi = n.astype(BF16)
    n_lo = (n - n_hi.astype(F32)).astype(BF16)
    logits = (jnp.dot(n_hi, wrh_ref[...], preferred_element_type=F32)
              + jnp.dot(n_lo, wrh_ref[...], preferred_element_type=F32)
              + jnp.dot(n_hi, wrl_ref[...], preferred_element_type=F32)) + br_ref[...]
    lane = lax.broadcasted_iota(I32, (tm, LANES), 1)
    lg = jnp.where(lane < N_EXPERTS, logits, -jnp.inf)

    sels, tops, idxs = [], [], []
    for _ in range(TOP_K):
        m = jnp.max(lg, axis=-1, keepdims=True)
        idx = jnp.min(jnp.where(lg == m, lane, LANES), axis=-1, keepdims=True)
        sel = lane == idx
        sels.append(sel)
        tops.append(m)
        idxs.append(idx)
        lg = jnp.where(sel, -jnp.inf, lg)
    exps = [jnp.exp(t - tops[0]) for t in tops]
    denom = exps[0] + exps[1] + exps[2] + exps[3]
    wts = [e / denom for e in exps]

    onehot = jnp.zeros((tm, LANES), F32)
    for sel in sels:
        onehot = onehot + sel.astype(F32)
    base = carry_ref[0:1, :] + jnp.dot(ltri_ref[...], onehot.astype(BF16), preferred_element_type=F32)
    meta = jnp.zeros((tm, LANES), I32)
    wlanes = jnp.zeros((tm, LANES), F32)
    for k in range(TOP_K):
        rank = jnp.sum(jnp.where(sels[k], base, 0.0), axis=-1, keepdims=True)
        meta = jnp.where(lane == k, idxs[k], meta)
        meta = jnp.where(lane == TOP_K + k, rank.astype(I32), meta)
        wlanes = jnp.where(lane == k, wts[k], wlanes)
    meta_ref[...] = meta
    wts_ref[...] = wlanes
    carry_ref[0:1, :] = carry_ref[0:1, :] + jnp.sum(onehot, axis=0, keepdims=True)
    cnt_ref[...] = jnp.broadcast_to(carry_ref[0:1, :], cnt_ref.shape).astype(I32)


def _tri_strict_lower(n):
    r = lax.broadcasted_iota(I32, (n, n), 0)
    c = lax.broadcasted_iota(I32, (n, n), 1)
    return (c < r).astype(BF16)


def _const_spec(shape):
    return pl.BlockSpec(shape, lambda i: (0,) * len(shape))


def _mixer(proj, proj_meta, attn_o, x2d, conv_w, wc, wa, wo, g_ffn, wr_hi, wr_lo, b_r, seq, tm):
    n_tok = x2d.shape[0]
    tiles_per_seq = seq // tm
    meta_blk = META_ROWS // (2 * SUBLANES) - 1
    in_specs = [
        pl.BlockSpec((tm, CONV_CH), lambda i: (i, 0)),
        pl.BlockSpec((tm, CONV_CH), lambda i: (i, 1)),
        pl.BlockSpec((tm, CONV_CH), lambda i: (i, 2)),
        pl.BlockSpec((tm, D_MODEL), lambda i: (i, 3)),
        pl.BlockSpec((tm, D_MODEL), lambda i: (i, 4)),
        pl.BlockSpec((tm, ATTN_WIDTH), lambda i: (i, 0)),
        pl.BlockSpec((tm, D_MODEL), lambda i: (i, 0)),
        pl.BlockSpec((2 * SUBLANES, CONV_CH), lambda i: (meta_blk, 0)),
        pl.BlockSpec((2 * SUBLANES, CONV_CH), lambda i: (meta_blk, 2)),
        _const_spec((SUBLANES, CONV_CH)),
        _const_spec((CONV_CH, D_MODEL)),
        _const_spec((ATTN_WIDTH, D_MODEL)),
        _const_spec((D_MODEL, D_MODEL)),
        _const_spec((1, D_MODEL)),
        _const_spec((D_MODEL, LANES)),
        _const_spec((D_MODEL, LANES)),
        _const_spec((1, LANES)),
        _const_spec((tm, tm)),
    ]
    out_specs = [
        pl.BlockSpec((tm, D_MODEL), lambda i: (i, 0)),
        pl.BlockSpec(_tok_shape(tm), lambda i: (0, i, 0)),
        pl.BlockSpec((tm, LANES), lambda i: (i, 0)),
        pl.BlockSpec((tm, LANES), lambda i: (i, 0)),
        _const_spec((SUBLANES, LANES)),
    ]
    out_shape = [
        jax.ShapeDtypeStruct((n_tok, D_MODEL), F32),
        jax.ShapeDtypeStruct(_tok_shape(n_tok), F32),
        jax.ShapeDtypeStruct((n_tok, LANES), I32),
        jax.ShapeDtypeStruct((n_tok, LANES), F32),
        jax.ShapeDtypeStruct((SUBLANES, LANES), I32),
    ]
    conv_w8 = jnp.pad(conv_w, ((0, SUBLANES - CONV_K), (0, 0)))
    return pl.pallas_call(
        functools.partial(_mixer_kernel, tm=tm, tiles_per_seq=tiles_per_seq),
        grid=(n_tok // tm,),
        in_specs=in_specs,
        out_specs=out_specs,
        out_shape=out_shape,
        scratch_shapes=[pltpu.VMEM((tm + SUBLANES, CONV_CH), F32), pltpu.VMEM((SUBLANES, LANES), F32)],
        compiler_params=_cparams(("arbitrary",)),
        name="mixer_out",
    )(proj, proj, proj, proj, proj, attn_o, x2d, proj_meta, proj_meta, conv_w8, wc, wa, wo,
      g_ffn.reshape(1, D_MODEL), wr_hi, wr_lo, b_r, _tri_strict_lower(tm))


def _row_slab(ref, row):
    return ref.at[:, pl.ds(pl.multiple_of(row * TOK_ROWS, TOK_ROWS), TOK_ROWS), :]


def _wait_rows(hbm_ref, vmem_or_hbm_ref, sem, n_rows):
    n_sub = n_rows * TOK_ROWS
    pltpu.make_async_copy(hbm_ref.at[:, pl.ds(0, n_sub), :], vmem_or_hbm_ref.at[:, pl.ds(0, n_sub), :],
                          sem).wait()


def _dispatch_kernel(eid_ref, rank_ref, offs_ref, npk_ref, xg_in_ref, xg_ref, sem, *, tm):
    del xg_in_ref

    def issue(t, c):
        for k in range(TOP_K):
            j = t * TOP_K + k
            dst = offs_ref[eid_ref[j]] + rank_ref[j]
            pltpu.make_async_copy(_row_slab(npk_ref, t), _row_slab(xg_ref, dst), sem).start(priority=k % 2)
        return c

    lax.fori_loop(0, tm, issue, 0, unroll=4)
    _wait_rows(xg_ref, xg_ref, sem, tm * TOP_K)


def _dispatch(eid_flat, rank_flat, offs, npk, xg0, tm):
    n_tok = npk.shape[1] // TOK_ROWS
    n_rows = xg0.shape[1] // TOK_ROWS
    smem_blk = pl.BlockSpec((tm * TOP_K,), lambda i: (i,), memory_space=pltpu.SMEM)
    return pl.pallas_call(
        functools.partial(_dispatch_kernel, tm=tm),
        grid=(n_tok // tm,),
        in_specs=[smem_blk, smem_blk,
                  pl.BlockSpec(memory_space=pltpu.SMEM),
                  pl.BlockSpec(_tok_shape(tm), lambda i: (0, i, 0)),
                  pl.BlockSpec(memory_space=pl.ANY)],
        out_specs=pl.BlockSpec(memory_space=pl.ANY),
        out_shape=jax.ShapeDtypeStruct(_tok_shape(n_rows), F32),
        scratch_shapes=[pltpu.SemaphoreType.DMA(())],
        input_output_aliases={4: 0},
        compiler_params=_cparams(("arbitrary",)),
        name="moe_dispatch",
    )(eid_flat, rank_flat, offs, npk, xg0)


def _expert_kernel(te_ref, tb_ref, tv_ref, nu_ref, x_ref, wg_ref, bg_ref, wu_ref, bu_ref, wd_ref, bd_ref,
                   o_ref, xs_ref, acc_ref, *, n_fc, tr, sub):
    del te_ref, tb_ref
    t = pl.program_id(0)
    f = pl.program_id(1)

    def tile_body(rows):
        @pl.when(f == 0)
        def _():
            for c in range(N_CHUNKS):
                xs_ref[0:rows, c * LANES:(c + 1) * LANES] = x_ref[_slab_idx(0, rows, c)].astype(BF16)
            acc_ref[0:rows, :] = jnp.zeros((rows, acc_ref.shape[1]), F32)

        x = xs_ref[0:rows, :]
        gate = jnp.dot(x, wg_ref[0, 0], preferred_element_type=F32) + bg_ref[0]
        up = jnp.dot(x, wu_ref[0, 0], preferred_element_type=F32) + bu_ref[0]
        gate = jnp.minimum(gate, SWIGLU_LIMIT)
        up = jnp.clip(up, -SWIGLU_LIMIT, SWIGLU_LIMIT)
        act = (up + 1.0) * (gate * jax.nn.sigmoid(SWIGLU_ALPHA * gate))
        acc_ref[0:rows, :] += jnp.dot(act.astype(BF16), wd_ref[0], preferred_element_type=F32)

        @pl.when(f == n_fc - 1)
        def _():
            _store_slabs(acc_ref[0:rows, :] + bd_ref[0], o_ref, rows)
            if rows < tr:
                o_ref[:, rows * TOK_ROWS:tr * TOK_ROWS, :] = jnp.zeros(
                    (TOK_HALVES, (tr - rows) * TOK_ROWS, LANES), F32)

    n_sub = (tv_ref[t] + sub - 1) // sub
    for nb in range(1, tr // sub + 1):
        pl.when(jnp.logical_and(t < nu_ref[0], n_sub == nb))(functools.partial(tile_body, nb * sub))


def _experts(tile_expert, tile_block, tile_valid, n_used, xg, w_gate, b_gate, w_up, b_up, w_down, b_down,
             tr, sub):
    n_rows = xg.shape[1] // TOK_ROWS
    n_exp, n_fc, d, fc = w_gate.shape
    dff = n_fc * fc
    grid_spec = pltpu.PrefetchScalarGridSpec(
        num_scalar_prefetch=4,
        grid=(n_rows // tr, n_fc),
        in_specs=[
            pl.BlockSpec(_tok_shape(tr), lambda t, f, te, tb, tv, nu: (0, tb[t], 0)),
            pl.BlockSpec((1, 1, d, fc), lambda t, f, te, tb, tv, nu: (te[t], f, 0, 0)),
            pl.BlockSpec((1, 1, fc), lambda t, f, te, tb, tv, nu: (te[t], 0, f)),
            pl.BlockSpec((1, 1, d, fc), lambda t, f, te, tb, tv, nu: (te[t], f, 0, 0)),
            pl.BlockSpec((1, 1, fc), lambda t, f, te, tb, tv, nu: (te[t], 0, f)),
            pl.BlockSpec((1, fc, d), lambda t, f, te, tb, tv, nu: (te[t], f, 0)),
            pl.BlockSpec((1, 1, d), lambda t, f, te, tb, tv, nu: (te[t], 0, 0)),
        ],
        out_specs=pl.BlockSpec(_tok_shape(tr), lambda t, f, te, tb, tv, nu: (0, tb[t], 0)),
        scratch_shapes=[pltpu.VMEM((tr, d), BF16), pltpu.VMEM((tr, d), F32)],
    )
    return pl.pallas_call(
        functools.partial(_expert_kernel, n_fc=n_fc, tr=tr, sub=sub),
        grid_spec=grid_spec,
        out_shape=jax.ShapeDtypeStruct(_tok_shape(n_rows), F32),
        input_output_aliases={4: 0},
        compiler_params=_cparams(("arbitrary", "arbitrary")),
        name="moe_experts",
    )(tile_expert, tile_block, tile_valid, n_used, xg, w_gate, b_gate.reshape(n_exp, 1, dff), w_up,
      b_up.reshape(n_exp, 1, dff), w_down, b_down.reshape(n_exp, 1, d))


def _combine_kernel(eid_ref, rank_ref, eid_next_ref, rank_next_ref, offs_ref, og_ref, wts_ref, h1_ref,
                    gfin_ref, out_ref, gbuf_ref, h2_ref, sems, *, tm):
    i = pl.program_id(0)
    slot = i % 2

    def gather(e_ref, r_ref, dst_slot):
        gdst = gbuf_ref.at[dst_slot]

        def issue(t, c):
            for k in range(TOP_K):
                j = t * TOP_K + k
                src = offs_ref[e_ref[j]] + r_ref[j]
                pltpu.make_async_copy(_row_slab(og_ref, src), _row_slab(gdst, k * tm + t),
                                      sems.at[dst_slot]).start(priority=k % 2)
            return c

        lax.fori_loop(0, tm, issue, 0, unroll=4)

    @pl.when(i == 0)
    def _():
        gather(eid_ref, rank_ref, 0)

    @pl.when(i + 1 < pl.num_programs(0))
    def _():
        gather(eid_next_ref, rank_next_ref, 1 - slot)

    gcur = gbuf_ref.at[slot]
    _wait_rows(og_ref, gcur, sems.at[slot], tm * TOP_K)

    wts = wts_ref[...]
    wk = [wts[:, k:k + 1] for k in range(TOP_K)]
    for c in range(N_CHUNKS):
        y = h1_ref[:, c * LANES:(c + 1) * LANES]
        for k in range(TOP_K):
            y = y + wk[k] * gcur[_slab_idx(k * tm, tm, c)]
        h2_ref[:, c * LANES:(c + 1) * LANES] = y
    h2 = h2_ref[...]
    ms = jnp.mean(h2 * h2, axis=-1, keepdims=True)
    out_ref[...] = h2 * lax.rsqrt(ms + RMS_EPS) * gfin_ref[...]


def _combine(eid_flat, rank_flat, offs, og, wts, h1, g_final, tm):
    n_tok = h1.shape[0]
    n_steps = n_tok // tm
    smem_blk = pl.BlockSpec((tm * TOP_K,), lambda i: (i,), memory_space=pltpu.SMEM)
    smem_next = pl.BlockSpec((tm * TOP_K,), lambda i: (jnp.minimum(i + 1, n_steps - 1),),
                             memory_space=pltpu.SMEM)
    return pl.pallas_call(
        functools.partial(_combine_kernel, tm=tm),
        grid=(n_steps,),
        in_specs=[smem_blk, smem_blk, smem_next, smem_next,
                  pl.BlockSpec(memory_space=pltpu.SMEM),
                  pl.BlockSpec(memory_space=pl.ANY),
                  pl.BlockSpec((tm, LANES), lambda i: (i, 0)),
                  pl.BlockSpec((tm, D_MODEL), lambda i: (i, 0)),
                  pl.BlockSpec((1, D_MODEL), lambda i: (0, 0))],
        out_specs=pl.BlockSpec((tm, D_MODEL), lambda i: (i, 0)),
        out_shape=jax.ShapeDtypeStruct((n_tok, D_MODEL), F32),
        scratch_shapes=[pltpu.VMEM((2,) + _tok_shape(tm * TOP_K), F32),
                        pltpu.VMEM((tm, D_MODEL), F32),
                        pltpu.SemaphoreType.DMA((2,))],
        compiler_params=_cparams(("arbitrary",)),
        name="moe_combine",
    )(eid_flat, rank_flat, eid_flat, rank_flat, offs, og, wts, h1, g_final.reshape(1, D_MODEL))


def _routing_tables(counts, tr, n_tiles):
    ntile = (counts + tr - 1) // tr
    tiles_cum = jnp.cumsum(ntile)
    offs = ((tiles_cum - ntile) * tr).astype(I32)
    n_used = tiles_cum[-1]
    t = jnp.minimum(jnp.arange(n_tiles, dtype=I32), n_used - 1)
    tile_expert = jnp.sum((tiles_cum[None, :] <= t[:, None]).astype(I32), axis=1)
    onehot = (tile_expert[:, None] == jnp.arange(counts.shape[0], dtype=I32)[None, :]).astype(I32)
    expert_end = jnp.sum(onehot * (offs + counts)[None, :], axis=1)
    tile_valid = jnp.clip(expert_end - t * tr, 1, tr).astype(I32)
    return offs, tile_expert, t.astype(I32), tile_valid, n_used.reshape(1).astype(I32)


def kernel(x, meta_tokens, g_mix, w_in, conv_w, w_conv_out, w_attn_out, w_o, g_ffn, w_router,
           b_router, w_gate, b_gate, w_up, b_up, w_down, b_down, g_final):
    assert g_mix.shape[0] == 1, "single-layer trunk"
    bsz, seq, d = x.shape
    n_tok = bsz * seq
    x2d = x.reshape(n_tok, d)

    n_tiles = (n_tok * TOP_K) // EXPERT_TR + N_EXPERTS
    hn = _rmsnorm_bf16(x2d, g_mix[0], RMS_TM)
    proj, xg0 = _in_proj(hn, w_in[0], PROJ_TM, PROJ_TN, zero_rows=n_tiles * EXPERT_TR)
    meta_pad = jnp.pad(meta_tokens.astype(x.dtype), ((META_ROWS - N_META, 0), (0, 0)))
    hn_meta = _rmsnorm_bf16(meta_pad, g_mix[0], META_ROWS)
    proj_meta = _in_proj(hn_meta, w_in[0], META_ROWS, PROJ_TN)

    attn_o, wg_bf, wu_bf, wd_bf = _attention(proj, proj_meta, w_gate[0], w_up[0], w_down[0],
                                             bsz, seq, ATT_TQ, ATT_TK, EXPERT_FC)

    wr = jnp.pad(w_router[0], ((0, 0), (0, LANES - N_EXPERTS)))
    wr_hi = wr.astype(BF16)
    wr_lo = (wr - wr_hi.astype(F32)).astype(BF16)
    b_r = jnp.pad(b_router[0], (0, LANES - N_EXPERTS)).reshape(1, LANES)
    h1, npk, meta, wts, cnt = _mixer(proj, proj_meta, attn_o, x2d, conv_w[0],
                                     w_conv_out[0].astype(BF16), w_attn_out[0].astype(BF16),
                                     w_o[0].astype(BF16), g_ffn[0], wr_hi, wr_lo, b_r, seq, MIX_TM)

    offs, tile_expert, tile_block, tile_valid, n_used = _routing_tables(cnt[0, :N_EXPERTS], EXPERT_TR, n_tiles)
    eid_flat = meta[:, 0:TOP_K].reshape(-1)
    rank_flat = meta[:, TOP_K:2 * TOP_K].reshape(-1)

    xg = _dispatch(eid_flat, rank_flat, offs, npk, xg0, DISPATCH_TM)
    og = _experts(tile_expert, tile_block, tile_valid, n_used, xg, wg_bf, b_gate[0], wu_bf, b_up[0],
                  wd_bf, b_down[0], EXPERT_TR, EXPERT_SUB)
    out = _combine(eid_flat, rank_flat, offs, og, wts, h1, g_final, COMBINE_TM)
    return out.reshape(bsz, seq, d)
```

```python
import functools
import math

import jax
import jax.numpy as jnp
from jax import lax
from jax.experimental import pallas as pl
from jax.experimental.pallas import tpu as pltpu

F32 = jnp.float32
BF16 = jnp.bfloat16
I32 = jnp.int32

D_MODEL = 2048
N_META = 16
N_HEADS = 8
HEAD_DIM = 128
ATTN_WIDTH = N_HEADS * HEAD_DIM
CONV_CH = D_MODEL // 2
CONV_K = 3
N_EXPERTS = 32
TOP_K = 4
D_FF = D_MODEL
SWIGLU_LIMIT = 7.0
SWIGLU_ALPHA = 1.702
RMS_EPS = 1e-5
IN_WIDTH = 3 * CONV_CH + 3 * ATTN_WIDTH + 2 * D_MODEL

LANES = 128
SUBLANES = 8
META_ROWS = 128
LOG2E = 1.4426950408889634
VMEM_LIMIT = 56 * 1024 * 1024

RMS_TM = 512
PROJ_TM = 1024
PROJ_TN = 1024
ATT_TQ = 512
ATT_TK = 256
MIX_TM = 256
DISPATCH_TM = 256
EXPERT_TR = 512
EXPERT_FC = 1024
EXPERT_SUB = 256
COMBINE_TM = 128
TOK_ROWS = SUBLANES
TOK_HALVES = D_MODEL // (TOK_ROWS * LANES)


def _tok_shape(n_rows):
    return (TOK_HALVES, n_rows * TOK_ROWS, LANES)


def _cparams(sem, vmem=VMEM_LIMIT):
    return pltpu.CompilerParams(dimension_semantics=sem, vmem_limit_bytes=vmem)


def _rmsnorm_kernel(x_ref, g_ref, o_ref):
    x = x_ref[...].astype(F32)
    ms = jnp.mean(x * x, axis=-1, keepdims=True)
    o_ref[...] = (x * lax.rsqrt(ms + RMS_EPS) * g_ref[...]).astype(o_ref.dtype)


def _rmsnorm_bf16(x, g, tm):
    m, d = x.shape
    return pl.pallas_call(
        _rmsnorm_kernel,
        grid=(m // tm,),
        in_specs=[pl.BlockSpec((tm, d), lambda i: (i, 0)),
                  pl.BlockSpec((1, d), lambda i: (0, 0))],
        out_specs=pl.BlockSpec((tm, d), lambda i: (i, 0)),
        out_shape=jax.ShapeDtypeStruct((m, d), BF16),
        compiler_params=_cparams(("arbitrary",)),
        name="rmsnorm",
    )(x, g.reshape(1, d))


ZSCALE = LOG2E / math.sqrt(HEAD_DIM)
Q_COL0 = 3 * CONV_CH


def _in_proj_kernel(x_ref, w_ref, o_ref, *rest, q_tile, zero_fill):
    wbf_ref = rest[-1]
    j = pl.program_id(0)

    @pl.when(pl.program_id(1) == 0)
    def _():
        wbf_ref[...] = w_ref[...].astype(BF16)

    acc = jnp.dot(x_ref[...], wbf_ref[...], preferred_element_type=F32)
    o_ref[...] = (acc * jnp.where(j == q_tile, ZSCALE, 1.0)).astype(o_ref.dtype)
    if zero_fill:
        rest[0][...] = jnp.zeros_like(rest[0])


def _in_proj(x, w, tm, tn, zero_rows=0):
    m, k = x.shape
    _, n = w.shape
    assert Q_COL0 % tn == 0 and ATTN_WIDTH == tn
    steps = (n // tn) * (m // tm)
    in_specs = [pl.BlockSpec((tm, k), lambda j, i: (i, 0)),
                pl.BlockSpec((k, tn), lambda j, i: (0, j))]
    out_specs = [pl.BlockSpec((tm, tn), lambda j, i: (i, j))]
    out_shape = [jax.ShapeDtypeStruct((m, n), BF16)]
    if zero_rows:
        nblk = max(d for d in range(1, steps + 1) if zero_rows % d == 0)
        n_i = m // tm
        out_specs.append(pl.BlockSpec(_tok_shape(zero_rows // nblk),
                                      lambda j, i: (0, jnp.minimum(j * n_i + i, nblk - 1), 0)))
        out_shape.append(jax.ShapeDtypeStruct(_tok_shape(zero_rows), F32))
    out = pl.pallas_call(
        functools.partial(_in_proj_kernel, q_tile=Q_COL0 // tn, zero_fill=bool(zero_rows)),
        grid=(n // tn, m // tm),
        in_specs=in_specs,
        out_specs=out_specs,
        out_shape=out_shape,
        scratch_shapes=[pltpu.VMEM((k, tn), BF16)],
        compiler_params=_cparams(("arbitrary", "arbitrary")),
        name="in_proj",
    )(x, w)
    return out if zero_rows else out[0]


def _attn_kernel(q_ref, k_ref, v_ref, km_ref, vm_ref, u_ref, um_ref, wg_ref, wu_ref, wd_ref,
                 o_ref, wgo_ref, wuo_ref, wdo_ref, vt_ref, vmt_ref, acc_ref, r_ref,
                 *, seq, tq, tk, fc):
    for c in range(D_FF // fc):
        wgo_ref[0, c] = wg_ref[:, c * fc:(c + 1) * fc].astype(BF16)
        wuo_ref[0, c] = wu_ref[:, c * fc:(c + 1) * fc].astype(BF16)
    wdo_ref[...] = wd_ref[...].astype(BF16)

    p = pl.program_id(2)
    n_q = seq // tq
    n_sub = tq // tk

    @pl.when(p == 0)
    def _():
        for j in range(seq // tk):
            vt_ref[j] = v_ref[0, j * tk:(j + 1) * tk, :].astype(F32).T.astype(BF16)
        vmt_ref[...] = vm_ref[...].astype(F32).T.astype(BF16)

    u = u_ref[...]
    krow = lax.broadcasted_iota(I32, (tk, tq), 0)
    qcol = lax.broadcasted_iota(I32, (tk, tq), 1)
    valid_meta = lax.broadcasted_iota(I32, (META_ROWS, tq), 0) >= META_ROWS - N_META

    def q_tile(qi):
        q0 = pl.multiple_of(qi * tq, tq)
        q = q_ref[0, pl.ds(q0, tq), :]
        acc_ref[...] = jnp.zeros_like(acc_ref)
        r_ref[...] = jnp.zeros_like(r_ref)

        def sweep(tiles):
            z2s = [lax.dot_general(k, q, (((1,), (1,)), ((), ())), preferred_element_type=F32)
                   for k, _, _, _ in tiles]
            cums = []
            for z2, (_, _, ut, mask) in zip(z2s, tiles):
                e = jnp.exp2(-jnp.abs(z2))
                s2 = jnp.maximum(z2, 0.0) + jnp.log2(1.0 + e)
                if mask is not None:
                    s2 = jnp.where(mask, s2, 0.0)
                cums.append(jnp.dot(ut, s2.astype(BF16), preferred_element_type=F32))
            r = r_ref[...]
            pv = None
            for z2, cum, (_, vt, _, mask) in zip(z2s, cums, tiles):
                w = jnp.exp2(z2 - cum - r)
                if mask is not None:
                    w = jnp.where(mask, w, 0.0)
                part = jnp.dot(vt, w.astype(BF16), preferred_element_type=F32)
                pv = part if pv is None else pv + part
                r = r + cum[0:1, :]
            acc_ref[...] += pv
            r_ref[...] = r

        def key_tile(j, mask):
            k0 = pl.multiple_of(j * tk, tk)
            return (k_ref[0, pl.ds(k0, tk), :], vt_ref[j], u, mask)

        sweep([key_tile(qi * n_sub + d, krow + d * tk < qcol) for d in reversed(range(n_sub))])

        def full(jj, c):
            base = (qi - 2 - 2 * jj) * n_sub
            sweep([key_tile(base + d, None) for d in reversed(range(2 * n_sub))])
            return c

        lax.fori_loop(0, qi // 2, full, 0)

        @pl.when(qi % 2 == 1)
        def _():
            sweep([key_tile(d, None) for d in reversed(range(n_sub))])

        sweep([(km_ref[...], vmt_ref[...], um_ref[...], valid_meta)])
        o_ref[0, pl.ds(q0, tq), :] = acc_ref[...].T.astype(o_ref.dtype)

    q_tile(p)
    q_tile(n_q - 1 - p)


def _tri_upper_incl(n):
    r = lax.broadcasted_iota(I32, (n, n), 0)
    c = lax.broadcasted_iota(I32, (n, n), 1)
    return (c >= r).astype(BF16)


def _attention(proj, proj_meta, w_gate, w_up, w_down, bsz, seq, tq, tk, fc):
    proj3 = proj.reshape(bsz, seq, IN_WIDTH)
    qb = Q_COL0 // HEAD_DIM
    kb, vb = qb + N_HEADS, qb + 2 * N_HEADS
    n_pair = seq // tq // 2
    n_exp, d, dff = w_gate.shape
    n_fc = dff // fc
    steps = bsz * N_HEADS * n_pair
    rc = (n_exp * d) // steps
    assert dff == d and rc * steps == n_exp * d and d % rc == 0 and rc % (2 * SUBLANES) == 0
    blk_per_e = d // rc

    def step(b, h, p):
        return (b * N_HEADS + h) * n_pair + p

    seq_spec = lambda col0: pl.BlockSpec((1, seq, HEAD_DIM), lambda b, h, p: (b, 0, col0 + h))
    w_in_spec = pl.BlockSpec((rc, dff), lambda b, h, p: (step(b, h, p), 0))
    wgu_out_spec = pl.BlockSpec((1, n_fc, rc, fc),
                                lambda b, h, p: (step(b, h, p) // blk_per_e, 0, step(b, h, p) % blk_per_e, 0))
    out, wg_bf, wu_bf, wd_bf = pl.pallas_call(
        functools.partial(_attn_kernel, seq=seq, tq=tq, tk=tk, fc=fc),
        grid=(bsz, N_HEADS, n_pair),
        in_specs=[seq_spec(qb), seq_spec(kb), seq_spec(vb),
                  pl.BlockSpec((META_ROWS, HEAD_DIM), lambda b, h, p: (0, kb + h)),
                  pl.BlockSpec((META_ROWS, HEAD_DIM), lambda b, h, p: (0, vb + h)),
                  pl.BlockSpec((tk, tk), lambda b, h, p: (0, 0)),
                  pl.BlockSpec((META_ROWS, META_ROWS), lambda b, h, p: (0, 0)),
                  w_in_spec, w_in_spec,
                  pl.BlockSpec((rc, d), lambda b, h, p: (step(b, h, p), 0))],
        out_specs=[pl.BlockSpec((1, seq, HEAD_DIM), lambda b, h, p: (b, 0, h)),
                   wgu_out_spec, wgu_out_spec,
                   pl.BlockSpec((rc, d), lambda b, h, p: (step(b, h, p), 0))],
        out_shape=[jax.ShapeDtypeStruct((bsz, seq, ATTN_WIDTH), BF16),
                   jax.ShapeDtypeStruct((n_exp, n_fc, d, fc), BF16),
                   jax.ShapeDtypeStruct((n_exp, n_fc, d, fc), BF16),
                   jax.ShapeDtypeStruct((n_exp * dff, d), BF16)],
        scratch_shapes=[pltpu.VMEM((seq // tk, HEAD_DIM, tk), BF16),
                        pltpu.VMEM((HEAD_DIM, META_ROWS), BF16),
                        pltpu.VMEM((HEAD_DIM, tq), F32),
                        pltpu.VMEM((1, tq), F32)],
        compiler_params=_cparams(("arbitrary", "arbitrary", "arbitrary")),
        name="stickbreak_attn",
    )(proj3, proj3, proj3, proj_meta, proj_meta, _tri_upper_incl(tk), _tri_upper_incl(META_ROWS),
      w_gate.reshape(n_exp * d, dff), w_up.reshape(n_exp * d, dff), w_down.reshape(n_exp * dff, d))
    return out.reshape(bsz * seq, ATTN_WIDTH), wg_bf, wu_bf, wd_bf.reshape(n_exp, dff, d)


def _slab_idx(first_tok, n_tok, c):
    h, s = divmod(c, TOK_ROWS)
    return (h, pl.ds(first_tok * TOK_ROWS + s, n_tok, stride=TOK_ROWS), slice(None))


N_CHUNKS = D_MODEL // LANES


def _store_slabs(vals, out_ref, n_tok):
    for c in range(N_CHUNKS):
        out_ref[_slab_idx(0, n_tok, c)] = vals[:, c * LANES:(c + 1) * LANES]


def _mixer_kernel(u_ref, bp_ref, cp_ref, gc_ref, ga_ref, o_ref, x_ref, um_ref, cm_ref,
                  convw_ref, wc_ref, wa_ref, wo_ref, gffn_ref, wrh_ref, wrl_ref, br_ref, ltri_ref,
                  h1_ref, npk_ref, meta_ref, wts_ref, cnt_ref,
                  cu_ref, carry_ref, *, tm, tiles_per_seq):
    i = pl.program_id(0)
    first = (i % tiles_per_seq) == 0

    @pl.when(i == 0)
    def _():
        carry_ref[...] = jnp.zeros_like(carry_ref)

    @pl.when(first)
    def _():
        cum = cm_ref[...].astype(F32) * um_ref[...].astype(F32)
        cu_ref[0:SUBLANES, :] = cum[SUBLANES:2 * SUBLANES, :]

    @pl.when(jnp.logical_not(first))
    def _():
        cu_ref[0:SUBLANES, :] = cu_ref[tm:tm + SUBLANES, :]

    cu = cp_ref[...].astype(F32) * u_ref[...].astype(F32)
    cu_ref[SUBLANES:tm + SUBLANES, :] = cu
    cw = convw_ref[...]
    conv = (cu_ref[SUBLANES - 2:tm + SUBLANES - 2, :] * cw[0:1, :]
            + cu_ref[SUBLANES - 1:tm + SUBLANES - 1, :] * cw[1:2, :]
            + cu * cw[2:3, :])
    y_conv = jnp.dot((bp_ref[...].astype(F32) * conv).astype(BF16), wc_ref[...],
                     preferred_element_type=F32)
    y_attn = jnp.dot(o_ref[...], wa_ref[...], preferred_element_type=F32)
    merged = (jax.nn.sigmoid(gc_ref[...].astype(F32)) * y_conv
              + jax.nn.sigmoid(ga_ref[...].astype(F32)) * y_attn)
    h1 = x_ref[...] + jnp.dot(merged.astype(BF16), wo_ref[...], preferred_element_type=F32)
    h1_ref[...] = h1

    ms = jnp.mean(h1 * h1, axis=-1, keepdims=True)
    n = h1 * lax.rsqrt(ms + RMS_EPS) * gffn_ref[...]
    _store_slabs(n, npk_ref, tm)

    n_hi = n.astype(BF16)
    n_lo = (n - n_hi.astype(F32)).astype(BF16)
    logits = (jnp.dot(n_hi, wrh_ref[...], preferred_element_type=F32)
              + jnp.dot(n_lo, wrh_ref[...], preferred_element_type=F32)
              + jnp.dot(n_hi, wrl_ref[...], preferred_element_type=F32)) + br_ref[...]
    lane = lax.broadcasted_iota(I32, (tm, LANES), 1)
    lg = jnp.where(lane < N_EXPERTS, logits, -jnp.inf)

    sels, tops, idxs = [], [], []
    for _ in range(TOP_K):
        m = jnp.max(lg, axis=-1, keepdims=True)
        idx = jnp.min(jnp.where(lg == m, lane, LANES), axis=-1, keepdims=True)
        sel = lane == idx
        sels.append(sel)
        tops.append(m)
        idxs.append(idx)
        lg = jnp.where(sel, -jnp.inf, lg)
    exps = [jnp.exp(t - tops[0]) for t in tops]
    denom = exps[0] + exps[1] + exps[2] + exps[3]
    wts = [e / denom for e in exps]

    onehot = jnp.zeros((tm, LANES), F32)
    for sel in sels:
        onehot = onehot + sel.astype(F32)
    base = carry_ref[0:1, :] + jnp.dot(ltri_ref[...], onehot.astype(BF16), preferred_element_type=F32)
    meta = jnp.zeros((tm, LANES), I32)
    wlanes = jnp.zeros((tm, LANES), F32)
    for k in range(TOP_K):
        rank = jnp.sum(jnp.where(sels[k], base, 0.0), axis=-1, keepdims=True)
        meta = jnp.where(lane == k, idxs[k], meta)
        meta = jnp.where(lane == TOP_K + k, rank.astype(I32), meta)
        wlanes = jnp.where(lane == k, wts[k], wlanes)
    meta_ref[...] = meta
    wts_ref[...] = wlanes
    carry_ref[0:1, :] = carry_ref[0:1, :] + jnp.sum(onehot, axis=0, keepdims=True)
    cnt_ref[...] = jnp.broadcast_to(carry_ref[0:1, :], cnt_ref.shape).astype(I32)


def _tri_strict_lower(n):
    r = lax.broadcasted_iota(I32, (n, n), 0)
    c = lax.broadcasted_iota(I32, (n, n), 1)
    return (c < r).astype(BF16)


def _const_spec(shape):
    return pl.BlockSpec(shape, lambda i: (0,) * len(shape))


def _mixer(proj, proj_meta, attn_o, x2d, conv_w, wc, wa, wo, g_ffn, wr_hi, wr_lo, b_r, seq, tm):
    n_tok = x2d.shape[0]
    tiles_per_seq = seq // tm
    meta_blk = META_ROWS // (2 * SUBLANES) - 1
    in_specs = [
        pl.BlockSpec((tm, CONV_CH), lambda i: (i, 0)),
        pl.BlockSpec((tm, CONV_CH), lambda i: (i, 1)),
        pl.BlockSpec((tm, CONV_CH), lambda i: (i, 2)),
        pl.BlockSpec((tm, D_MODEL), lambda i: (i, 3)),
        pl.BlockSpec((tm, D_MODEL), lambda i: (i, 4)),
        pl.BlockSpec((tm, ATTN_WIDTH), lambda i: (i, 0)),
        pl.BlockSpec((tm, D_MODEL), lambda i: (i, 0)),
        pl.BlockSpec((2 * SUBLANES, CONV_CH), lambda i: (meta_blk, 0)),
        pl.BlockSpec((2 * SUBLANES, CONV_CH), lambda i: (meta_blk, 2)),
        _const_spec((SUBLANES, CONV_CH)),
        _const_spec((CONV_CH, D_MODEL)),
        _const_spec((ATTN_WIDTH, D_MODEL)),
        _const_spec((D_MODEL, D_MODEL)),
        _const_spec((1, D_MODEL)),
        _const_spec((D_MODEL, LANES)),
        _const_spec((D_MODEL, LANES)),
        _const_spec((1, LANES)),
        _const_spec((tm, tm)),
    ]
    out_specs = [
        pl.BlockSpec((tm, D_MODEL), lambda i: (i, 0)),
        pl.BlockSpec(_tok_shape(tm), lambda i: (0, i, 0)),
        pl.BlockSpec((tm, LANES), lambda i: (i, 0)),
        pl.BlockSpec((tm, LANES), lambda i: (i, 0)),
        _const_spec((SUBLANES, LANES)),
    ]
    out_shape = [
        jax.ShapeDtypeStruct((n_tok, D_MODEL), F32),
        jax.ShapeDtypeStruct(_tok_shape(n_tok), F32),
        jax.ShapeDtypeStruct((n_tok, LANES), I32),
        jax.ShapeDtypeStruct((n_tok, LANES), F32),
        jax.ShapeDtypeStruct((SUBLANES, LANES), I32),
    ]
    conv_w8 = jnp.pad(conv_w, ((0, SUBLANES - CONV_K), (0, 0)))
    return pl.pallas_call(
        functools.partial(_mixer_kernel, tm=tm, tiles_per_seq=tiles_per_seq),
        grid=(n_tok // tm,),
        in_specs=in_specs,
        out_specs=out_specs,
        out_shape=out_shape,
        scratch_shapes=[pltpu.VMEM((tm + SUBLANES, CONV_CH), F32), pltpu.VMEM((SUBLANES, LANES), F32)],
        compiler_params=_cparams(("arbitrary",)),
        name="mixer_out",
    )(proj, proj, proj, proj, proj, attn_o, x2d, proj_meta, proj_meta, conv_w8, wc, wa, wo,
      g_ffn.reshape(1, D_MODEL), wr_hi, wr_lo, b_r, _tri_strict_lower(tm))


def _row_slab(ref, row):
    return ref.at[:, pl.ds(pl.multiple_of(row * TOK_ROWS, TOK_ROWS), TOK_ROWS), :]


def _wait_rows(hbm_ref, vmem_or_hbm_ref, sem, n_rows):
    n_sub = n_rows * TOK_ROWS
    pltpu.make_async_copy(hbm_ref.at[:, pl.ds(0, n_sub), :], vmem_or_hbm_ref.at[:, pl.ds(0, n_sub), :],
                          sem).wait()


def _dispatch_kernel(eid_ref, rank_ref, offs_ref, npk_ref, xg_in_ref, xg_ref, sem, *, tm):
    del xg_in_ref

    def issue(t, c):
        for k in range(TOP_K):
            j = t * TOP_K + k
            dst = offs_ref[eid_ref[j]] + rank_ref[j]
            pltpu.make_async_copy(_row_slab(npk_ref, t), _row_slab(xg_ref, dst), sem).start(priority=k % 2)
        return c

    lax.fori_loop(0, tm, issue, 0, unroll=4)
    _wait_rows(xg_ref, xg_ref, sem, tm * TOP_K)


def _dispatch(eid_flat, rank_flat, offs, npk, xg0, tm):
    n_tok = npk.shape[1] // TOK_ROWS
    n_rows = xg0.shape[1] // TOK_ROWS
    smem_blk = pl.BlockSpec((tm * TOP_K,), lambda i: (i,), memory_space=pltpu.SMEM)
    return pl.pallas_call(
        functools.partial(_dispatch_kernel, tm=tm),
        grid=(n_tok // tm,),
        in_specs=[smem_blk, smem_blk,
                  pl.BlockSpec(memory_space=pltpu.SMEM),
                  pl.BlockSpec(_tok_shape(tm), lambda i: (0, i, 0)),
                  pl.BlockSpec(memory_space=pl.ANY)],
        out_specs=pl.BlockSpec(memory_space=pl.ANY),
        out_shape=jax.ShapeDtypeStruct(_tok_shape(n_rows), F32),
        scratch_shapes=[pltpu.SemaphoreType.DMA(())],
        input_output_aliases={4: 0},
        compiler_params=_cparams(("arbitrary",)),
        name="moe_dispatch",
    )(eid_flat, rank_flat, offs, npk, xg0)


def _expert_kernel(te_ref, tb_ref, tv_ref, nu_ref, x_ref, wg_ref, bg_ref, wu_ref, bu_ref, wd_ref, bd_ref,
                   o_ref, xs_ref, acc_ref, *, n_fc, tr, sub):
    del te_ref, tb_ref
    t = pl.program_id(0)
    f = pl.program_id(1)

    def tile_body(rows):
        @pl.when(f == 0)
        def _():
            for c in range(N_CHUNKS):
                xs_ref[0:rows, c * LANES:(c + 1) * LANES] = x_ref[_slab_idx(0, rows, c)].astype(BF16)
            acc_ref[0:rows, :] = jnp.zeros((rows, acc_ref.shape[1]), F32)

        x = xs_ref[0:rows, :]
        gate = jnp.dot(x, wg_ref[0, 0], preferred_element_type=F32) + bg_ref[0]
        up = jnp.dot(x, wu_ref[0, 0], preferred_element_type=F32) + bu_ref[0]
        gate = jnp.minimum(gate, SWIGLU_LIMIT)
        up = jnp.clip(up, -SWIGLU_LIMIT, SWIGLU_LIMIT)
        act = (up + 1.0) * (gate * jax.nn.sigmoid(SWIGLU_ALPHA * gate))
        acc_ref[0:rows, :] += jnp.dot(act.astype(BF16), wd_ref[0], preferred_element_type=F32)

        @pl.when(f == n_fc - 1)
        def _():
            _store_slabs(acc_ref[0:rows, :] + bd_ref[0], o_ref, rows)
            if rows < tr:
                o_ref[:, rows * TOK_ROWS:tr * TOK_ROWS, :] = jnp.zeros(
                    (TOK_HALVES, (tr - rows) * TOK_ROWS, LANES), F32)

    n_sub = (tv_ref[t] + sub - 1) // sub
    for nb in range(1, tr // sub + 1):
        pl.when(jnp.logical_and(t < nu_ref[0], n_sub == nb))(functools.partial(tile_body, nb * sub))


def _experts(tile_expert, tile_block, tile_valid, n_used, xg, w_gate, b_gate, w_up, b_up, w_down, b_down,
             tr, sub):
    n_rows = xg.shape[1] // TOK_ROWS
    n_exp, n_fc, d, fc = w_gate.shape
    dff = n_fc * fc
    grid_spec = pltpu.PrefetchScalarGridSpec(
        num_scalar_prefetch=4,
        grid=(n_rows // tr, n_fc),
        in_specs=[
            pl.BlockSpec(_tok_shape(tr), lambda t, f, te, tb, tv, nu: (0, tb[t], 0)),
            pl.BlockSpec((1, 1, d, fc), lambda t, f, te, tb, tv, nu: (te[t], f, 0, 0)),
            pl.BlockSpec((1, 1, fc), lambda t, f, te, tb, tv, nu: (te[t], 0, f)),
            pl.BlockSpec((1, 1, d, fc), lambda t, f, te, tb, tv, nu: (te[t], f, 0, 0)),
            pl.BlockSpec((1, 1, fc), lambda t, f, te, tb, tv, nu: (te[t], 0, f)),
            pl.BlockSpec((1, fc, d), lambda t, f, te, tb, tv, nu: (te[t], f, 0)),
            pl.BlockSpec((1, 1, d), lambda t, f, te, tb, tv, nu: (te[t], 0, 0)),
        ],
        out_specs=pl.BlockSpec(_tok_shape(tr), lambda t, f, te, tb, tv, nu: (0, tb[t], 0)),
        scratch_shapes=[pltpu.VMEM((tr, d), BF16), pltpu.VMEM((tr, d), F32)],
    )
    return pl.pallas_call(
        functools.partial(_expert_kernel, n_fc=n_fc, tr=tr, sub=sub),
        grid_spec=grid_spec,
        out_shape=jax.ShapeDtypeStruct(_tok_shape(n_rows), F32),
        input_output_aliases={4: 0},
        compiler_params=_cparams(("arbitrary", "arbitrary")),
        name="moe_experts",
    )(tile_expert, tile_block, tile_valid, n_used, xg, w_gate, b_gate.reshape(n_exp, 1, dff), w_up,
      b_up.reshape(n_exp, 1, dff), w_down, b_down.reshape(n_exp, 1, d))


def _combine_kernel(eid_ref, rank_ref, eid_next_ref, rank_next_ref, offs_ref, og_ref, wts_ref, h1_ref,
                    gfin_ref, out_ref, gbuf_ref, h2_ref, sems, *, tm):
    i = pl.program_id(0)
    slot = i % 2

    def gather(e_ref, r_ref, dst_slot):
        gdst = gbuf_ref.at[dst_slot]

        def issue(t, c):
            for k in range(TOP_K):
                j = t * TOP_K + k
                src = offs_ref[e_ref[j]] + r_ref[j]
                pltpu.make_async_copy(_row_slab(og_ref, src), _row_slab(gdst, k * tm + t),
                                      sems.at[dst_slot]).start(priority=k % 2)
            return c

        lax.fori_loop(0, tm, issue, 0, unroll=4)

    @pl.when(i == 0)
    def _():
        gather(eid_ref, rank_ref, 0)

    @pl.when(i + 1 < pl.num_programs(0))
    def _():
        gather(eid_next_ref, rank_next_ref, 1 - slot)

    gcur = gbuf_ref.at[slot]
    _wait_rows(og_ref, gcur, sems.at[slot], tm * TOP_K)

    wts = wts_ref[...]
    wk = [wts[:, k:k + 1] for k in range(TOP_K)]
    for c in range(N_CHUNKS):
        y = h1_ref[:, c * LANES:(c + 1) * LANES]
        for k in range(TOP_K):
            y = y + wk[k] * gcur[_slab_idx(k * tm, tm, c)]
        h2_ref[:, c * LANES:(c + 1) * LANES] = y
    h2 = h2_ref[...]
    ms = jnp.mean(h2 * h2, axis=-1, keepdims=True)
    out_ref[...] = h2 * lax.rsqrt(ms + RMS_EPS) * gfin_ref[...]


def _combine(eid_flat, rank_flat, offs, og, wts, h1, g_final, tm):
    n_tok = h1.shape[0]
    n_steps = n_tok // tm
    smem_blk = pl.BlockSpec((tm * TOP_K,), lambda i: (i,), memory_space=pltpu.SMEM)
    smem_next = pl.BlockSpec((tm * TOP_K,), lambda i: (jnp.minimum(i + 1, n_steps - 1),),
                             memory_space=pltpu.SMEM)
    return pl.pallas_call(
        functools.partial(_combine_kernel, tm=tm),
        grid=(n_steps,),
        in_specs=[smem_blk, smem_blk, smem_next, smem_next,
                  pl.BlockSpec(memory_space=pltpu.SMEM),
                  pl.BlockSpec(memory_space=pl.ANY),
                  pl.BlockSpec((tm, LANES), lambda i: (i, 0)),
                  pl.BlockSpec((tm, D_MODEL), lambda i: (i, 0)),
                  pl.BlockSpec((1, D_MODEL), lambda i: (0, 0))],
        out_specs=pl.BlockSpec((tm, D_MODEL), lambda i: (i, 0)),
        out_shape=jax.ShapeDtypeStruct((n_tok, D_MODEL), F32),
        scratch_shapes=[pltpu.VMEM((2,) + _tok_shape(tm * TOP_K), F32),
                        pltpu.VMEM((tm, D_MODEL), F32),
                        pltpu.SemaphoreType.DMA((2,))],
        compiler_params=_cparams(("arbitrary",)),
        name="moe_combine",
    )(eid_flat, rank_flat, eid_flat, rank_flat, offs, og, wts, h1, g_final.reshape(1, D_MODEL))


def _routing_tables(counts, tr, n_tiles):
    ntile = (counts + tr - 1) // tr
    tiles_cum = jnp.cumsum(ntile)
    offs = ((tiles_cum - ntile) * tr).astype(I32)
    n_used = tiles_cum[-1]
    t = jnp.minimum(jnp.arange(n_tiles, dtype=I32), n_used - 1)
    tile_expert = jnp.sum((tiles_cum[None, :] <= t[:, None]).astype(I32), axis=1)
    onehot = (tile_expert[:, None] == jnp.arange(counts.shape[0], dtype=I32)[None, :]).astype(I32)
    expert_end = jnp.sum(onehot * (offs + counts)[None, :], axis=1)
    tile_valid = jnp.clip(expert_end - t * tr, 1, tr).astype(I32)
    return offs, tile_expert, t.astype(I32), tile_valid, n_used.reshape(1).astype(I32)


def kernel(x, meta_tokens, g_mix, w_in, conv_w, w_conv_out, w_attn_out, w_o, g_ffn, w_router,
           b_router, w_gate, b_gate, w_up, b_up, w_down, b_down, g_final):
    assert g_mix.shape[0] == 1, "single-layer trunk"
    bsz, seq, d = x.shape
    n_tok = bsz * seq
    x2d = x.reshape(n_tok, d)

    n_tiles = (n_tok * TOP_K) // EXPERT_TR + N_EXPERTS
    hn = _rmsnorm_bf16(x2d, g_mix[0], RMS_TM)
    proj, xg0 = _in_proj(hn, w_in[0], PROJ_TM, PROJ_TN, zero_rows=n_tiles * EXPERT_TR)
    meta_pad = jnp.pad(meta_tokens.astype(x.dtype), ((META_ROWS - N_META, 0), (0, 0)))
    hn_meta = _rmsnorm_bf16(meta_pad, g_mix[0], META_ROWS)
    proj_meta = _in_proj(hn_meta, w_in[0], META_ROWS, PROJ_TN)

    attn_o, wg_bf, wu_bf, wd_bf = _attention(proj, proj_meta, w_gate[0], w_up[0], w_down[0],
                                             bsz, seq, ATT_TQ, ATT_TK, EXPERT_FC)

    wr = jnp.pad(w_router[0], ((0, 0), (0, LANES - N_EXPERTS)))
    wr_hi = wr.astype(BF16)
    wr_lo = (wr - wr_hi.astype(F32)).astype(BF16)
    b_r = jnp.pad(b_router[0], (0, LANES - N_EXPERTS)).reshape(1, LANES)
    h1, npk, meta, wts, cnt = _mixer(proj, proj_meta, attn_o, x2d, conv_w[0],
                                     w_conv_out[0].astype(BF16), w_attn_out[0].astype(BF16),
                                     w_o[0].astype(BF16), g_ffn[0], wr_hi, wr_lo, b_r, seq, MIX_TM)

    offs, tile_expert, tile_block, tile_valid, n_used = _routing_tables(cnt[0, :N_EXPERTS], EXPERT_TR, n_tiles)
    eid_flat = meta[:, 0:TOP_K].reshape(-1)
    rank_flat = meta[:, TOP_K:2 * TOP_K].reshape(-1)

    xg = _dispatch(eid_flat, rank_flat, offs, npk, xg0, DISPATCH_TM)
    og = _experts(tile_expert, tile_block, tile_valid, n_used, xg, wg_bf, b_gate[0], wu_bf, b_up[0],
                  wd_bf, b_down[0], EXPERT_TR, EXPERT_SUB)
    out = _combine(eid_flat, rank_flat, offs, og, wts, h1, g_final, COMBINE_TM)
    return out.reshape(bsz, seq, d)
```

```python
import functools
import math

import jax
import jax.numpy as jnp
from jax import lax
from jax.experimental import pallas as pl
from jax.experimental.pallas import tpu as pltpu

F32 = jnp.float32
BF16 = jnp.bfloat16
I32 = jnp.int32

D_MODEL = 2048
N_META = 16
N_HEADS = 8
HEAD_DIM = 128
ATTN_WIDTH = N_HEADS * HEAD_DIM
CONV_CH = D_MODEL // 2
CONV_K = 3
N_EXPERTS = 32
TOP_K = 4
D_FF = D_MODEL
SWIGLU_LIMIT = 7.0
SWIGLU_ALPHA = 1.702
RMS_EPS = 1e-5
IN_WIDTH = 3 * CONV_CH + 3 * ATTN_WIDTH + 2 * D_MODEL

LANES = 128
SUBLANES = 8
META_ROWS = 128
LOG2E = 1.4426950408889634
VMEM_LIMIT = 56 * 1024 * 1024
EXPERT_VMEM_LIMIT = 60 * 1024 * 1024

RMS_TM = 512
PROJ_TM = 1024
PROJ_TN = 1024
ATT_TQ = 512
ATT_TK = 256
MIX_TM = 256
DISPATCH_TM = 256
EXPERT_TR = 768
EXPERT_FC = 1024
EXPERT_SUB = 256
COMBINE_TM = 256
TOK_ROWS = SUBLANES
TOK_HALVES = D_MODEL // (TOK_ROWS * LANES)


def _tok_shape(n_rows):
    return (TOK_HALVES, n_rows * TOK_ROWS, LANES)


def _cparams(sem, vmem=VMEM_LIMIT):
    return pltpu.CompilerParams(dimension_semantics=sem, vmem_limit_bytes=vmem)


def _rmsnorm_kernel(x_ref, g_ref, o_ref):
    x = x_ref[...].astype(F32)
    ms = jnp.mean(x * x, axis=-1, keepdims=True)
    o_ref[...] = (x * lax.rsqrt(ms + RMS_EPS) * g_ref[...]).astype(o_ref.dtype)


def _rmsnorm_bf16(x, g, tm):
    m, d = x.shape
    return pl.pallas_call(
        _rmsnorm_kernel,
        grid=(m // tm,),
        in_specs=[pl.BlockSpec((tm, d), lambda i: (i, 0)),
                  pl.BlockSpec((1, d), lambda i: (0, 0))],
        out_specs=pl.BlockSpec((tm, d), lambda i: (i, 0)),
        out_shape=jax.ShapeDtypeStruct((m, d), BF16),
        compiler_params=_cparams(("arbitrary",)),
        name="rmsnorm",
    )(x, g.reshape(1, d))


ZSCALE = LOG2E / math.sqrt(HEAD_DIM)
Q_COL0 = 3 * CONV_CH


def _in_proj_kernel(x_ref, w_ref, o_ref, *rest, q_tile, zero_fill):
    wbf_ref = rest[-1]
    j = pl.program_id(0)

    @pl.when(pl.program_id(1) == 0)
    def _():
        wbf_ref[...] = w_ref[...].astype(BF16)

    acc = jnp.dot(x_ref[...], wbf_ref[...], preferred_element_type=F32)
    o_ref[...] = (acc * jnp.where(j == q_tile, ZSCALE, 1.0)).astype(o_ref.dtype)
    if zero_fill:
        rest[0][...] = jnp.zeros_like(rest[0])


def _in_proj(x, w, tm, tn, zero_rows=0):
    m, k = x.shape
    _, n = w.shape
    assert Q_COL0 % tn == 0 and ATTN_WIDTH == tn
    steps = (n // tn) * (m // tm)
    in_specs = [pl.BlockSpec((tm, k), lambda j, i: (i, 0)),
                pl.BlockSpec((k, tn), lambda j, i: (0, j))]
    out_specs = [pl.BlockSpec((tm, tn), lambda j, i: (i, j))]
    out_shape = [jax.ShapeDtypeStruct((m, n), BF16)]
    if zero_rows:
        nblk = max(d for d in range(1, steps + 1) if zero_rows % d == 0)
        n_i = m // tm
        out_specs.append(pl.BlockSpec(_tok_shape(zero_rows // nblk),
                                      lambda j, i: (0, jnp.minimum(j * n_i + i, nblk - 1), 0)))
        out_shape.append(jax.ShapeDtypeStruct(_tok_shape(zero_rows), F32))
    out = pl.pallas_call(
        functools.partial(_in_proj_kernel, q_tile=Q_COL0 // tn, zero_fill=bool(zero_rows)),
        grid=(n // tn, m // tm),
        in_specs=in_specs,
        out_specs=out_specs,
        out_shape=out_shape,
        scratch_shapes=[pltpu.VMEM((k, tn), BF16)],
        compiler_params=_cparams(("arbitrary", "arbitrary")),
        name="in_proj",
    )(x, w)
    return out if zero_rows else out[0]


def _attn_kernel(q_ref, k_ref, v_ref, km_ref, vm_ref, u_ref, um_ref, wg_ref, wu_ref, wd_ref,
                 o_ref, wgo_ref, wuo_ref, wdo_ref, vt_ref, vmt_ref, acc_ref, r_ref,
                 *, seq, tq, tk, fc):
    for c in range(D_FF // fc):
        wgo_ref[0, c] = wg_ref[:, c * fc:(c + 1) * fc].astype(BF16)
        wuo_ref[0, c] = wu_ref[:, c * fc:(c + 1) * fc].astype(BF16)
    wdo_ref[...] = wd_ref[...].astype(BF16)

    p = pl.program_id(2)
    n_q = seq // tq
    n_sub = tq // tk

    @pl.when(p == 0)
    def _():
        for j in range(seq // tk):
            vt_ref[j] = v_ref[0, j * tk:(j + 1) * tk, :].astype(F32).T.astype(BF16)
        vmt_ref[...] = vm_ref[...].astype(F32).T.astype(BF16)

    u = u_ref[...]
    krow = lax.broadcasted_iota(I32, (tk, tq), 0)
    qcol = lax.broadcasted_iota(I32, (tk, tq), 1)
    valid_meta = lax.broadcasted_iota(I32, (META_ROWS, tq), 0) >= META_ROWS - N_META

    def q_tile(qi):
        q0 = pl.multiple_of(qi * tq, tq)
        q = q_ref[0, pl.ds(q0, tq), :]
        acc_ref[...] = jnp.zeros_like(acc_ref)
        r_ref[...] = jnp.zeros_like(r_ref)

        def sweep(tiles):
            z2s = [lax.dot_general(k, q, (((1,), (1,)), ((), ())), preferred_element_type=F32)
                   for k, _, _, _ in tiles]
            cums = []
            for z2, (_, _, ut, mask) in zip(z2s, tiles):
                e = jnp.exp2(-jnp.abs(z2))
                s2 = jnp.maximum(z2, 0.0) + jnp.log2(1.0 + e)
                if mask is not None:
                    s2 = jnp.where(mask, s2, 0.0)
                cums.append(jnp.dot(ut, s2.astype(BF16), preferred_element_type=F32))
            r = r_ref[...]
            pv = None
            for z2, cum, (_, vt, _, mask) in zip(z2s, cums, tiles):
                w = jnp.exp2(z2 - cum - r)
                if mask is not None:
                    w = jnp.where(mask, w, 0.0)
                part = jnp.dot(vt, w.astype(BF16), preferred_element_type=F32)
                pv = part if pv is None else pv + part
                r = r + cum[0:1, :]
            acc_ref[...] += pv
            r_ref[...] = r

        def key_tile(j, mask):
            k0 = pl.multiple_of(j * tk, tk)
            return (k_ref[0, pl.ds(k0, tk), :], vt_ref[j], u, mask)

        sweep([key_tile(qi * n_sub + d, krow + d * tk < qcol) for d in reversed(range(n_sub))])

        def full(jj, c):
            base = (qi - 2 - 2 * jj) * n_sub
            sweep([key_tile(base + d, None) for d in reversed(range(2 * n_sub))])
            return c

        lax.fori_loop(0, qi // 2, full, 0)

        @pl.when(qi % 2 == 1)
        def _():
            sweep([key_tile(d, None) for d in reversed(range(n_sub))])

        sweep([(km_ref[...], vmt_ref[...], um_ref[...], valid_meta)])
        o_ref[0, pl.ds(q0, tq), :] = acc_ref[...].T.astype(o_ref.dtype)

    q_tile(p)
    q_tile(n_q - 1 - p)


def _tri_upper_incl(n):
    r = lax.broadcasted_iota(I32, (n, n), 0)
    c = lax.broadcasted_iota(I32, (n, n), 1)
    return (c >= r).astype(BF16)


def _attention(proj, proj_meta, w_gate, w_up, w_down, bsz, seq, tq, tk, fc):
    proj3 = proj.reshape(bsz, seq, IN_WIDTH)
    qb = Q_COL0 // HEAD_DIM
    kb, vb = qb + N_HEADS, qb + 2 * N_HEADS
    n_pair = seq // tq // 2
    n_exp, d, dff = w_gate.shape
    n_fc = dff // fc
    steps = bsz * N_HEADS * n_pair
    rc = (n_exp * d) // steps
    assert dff == d and rc * steps == n_exp * d and d % rc == 0 and rc % (2 * SUBLANES) == 0
    blk_per_e = d // rc

    def step(b, h, p):
        return (b * N_HEADS + h) * n_pair + p

    seq_spec = lambda col0: pl.BlockSpec((1, seq, HEAD_DIM), lambda b, h, p: (b, 0, col0 + h))
    w_in_spec = pl.BlockSpec((rc, dff), lambda b, h, p: (step(b, h, p), 0))
    wgu_out_spec = pl.BlockSpec((1, n_fc, rc, fc),
                                lambda b, h, p: (step(b, h, p) // blk_per_e, 0, step(b, h, p) % blk_per_e, 0))
    out, wg_bf, wu_bf, wd_bf = pl.pallas_call(
        functools.partial(_attn_kernel, seq=seq, tq=tq, tk=tk, fc=fc),
        grid=(bsz, N_HEADS, n_pair),
        in_specs=[seq_spec(qb), seq_spec(kb), seq_spec(vb),
                  pl.BlockSpec((META_ROWS, HEAD_DIM), lambda b, h, p: (0, kb + h)),
                  pl.BlockSpec((META_ROWS, HEAD_DIM), lambda b, h, p: (0, vb + h)),
                  pl.BlockSpec((tk, tk), lambda b, h, p: (0, 0)),
                  pl.BlockSpec((META_ROWS, META_ROWS), lambda b, h, p: (0, 0)),
                  w_in_spec, w_in_spec,
                  pl.BlockSpec((rc, d), lambda b, h, p: (step(b, h, p), 0))],
        out_specs=[pl.BlockSpec((1, seq, HEAD_DIM), lambda b, h, p: (b, 0, h)),
                   wgu_out_spec, wgu_out_spec,
                   pl.BlockSpec((rc, d), lambda b, h, p: (step(b, h, p), 0))],
        out_shape=[jax.ShapeDtypeStruct((bsz, seq, ATTN_WIDTH), BF16),
                   jax.ShapeDtypeStruct((n_exp, n_fc, d, fc), BF16),
                   jax.ShapeDtypeStruct((n_exp, n_fc, d, fc), BF16),
                   jax.ShapeDtypeStruct((n_exp * dff, d), BF16)],
        scratch_shapes=[pltpu.VMEM((seq // tk, HEAD_DIM, tk), BF16),
                        pltpu.VMEM((HEAD_DIM, META_ROWS), BF16),
                        pltpu.VMEM((HEAD_DIM, tq), F32),
                        pltpu.VMEM((1, tq), F32)],
        compiler_params=_cparams(("arbitrary", "arbitrary", "arbitrary")),
        name="stickbreak_attn",
    )(proj3, proj3, proj3, proj_meta, proj_meta, _tri_upper_incl(tk), _tri_upper_incl(META_ROWS),
      w_gate.reshape(n_exp * d, dff), w_up.reshape(n_exp * d, dff), w_down.reshape(n_exp * dff, d))
    return out.reshape(bsz * seq, ATTN_WIDTH), wg_bf, wu_bf, wd_bf.reshape(n_exp, dff, d)


def _slab_idx(first_tok, n_tok, c):
    h, s = divmod(c, TOK_ROWS)
    return (h, pl.ds(first_tok * TOK_ROWS + s, n_tok, stride=TOK_ROWS), slice(None))


N_CHUNKS = D_MODEL // LANES


def _store_slabs(vals, out_ref, n_tok):
    for c in range(N_CHUNKS):
        out_ref[_slab_idx(0, n_tok, c)] = vals[:, c * LANES:(c + 1) * LANES]


def _mixer_kernel(u_ref, bp_ref, cp_ref, gc_ref, ga_ref, o_ref, x_ref, um_ref, cm_ref,
                  convw_ref, wc_ref, wa_ref, wo_ref, gffn_ref, wrh_ref, wrl_ref, br_ref, ltri_ref,
                  h1_ref, npk_ref, meta_ref, wts_ref, cnt_ref,
                  cu_ref, carry_ref, *, tm, tiles_per_seq):
    i = pl.program_id(0)
    first = (i % tiles_per_seq) == 0

    @pl.when(i == 0)
    def _():
        carry_ref[...] = jnp.zeros_like(carry_ref)

    @pl.when(first)
    def _():
        cum = cm_ref[...].astype(F32) * um_ref[...].astype(F32)
        cu_ref[0:SUBLANES, :] = cum[SUBLANES:2 * SUBLANES, :]

    @pl.when(jnp.logical_not(first))
    def _():
        cu_ref[0:SUBLANES, :] = cu_ref[tm:tm + SUBLANES, :]

    cu = cp_ref[...].astype(F32) * u_ref[...].astype(F32)
    cu_ref[SUBLANES:tm + SUBLANES, :] = cu
    cw = convw_ref[...]
    conv = (cu_ref[SUBLANES - 2:tm + SUBLANES - 2, :] * cw[0:1, :]
            + cu_ref[SUBLANES - 1:tm + SUBLANES - 1, :] * cw[1:2, :]
            + cu * cw[2:3, :])
    y_conv = jnp.dot((bp_ref[...].astype(F32) * conv).astype(BF16), wc_ref[...],
                     preferred_element_type=F32)
    y_attn = jnp.dot(o_ref[...], wa_ref[...], preferred_element_type=F32)
    merged = (jax.nn.sigmoid(gc_ref[...].astype(F32)) * y_conv
              + jax.nn.sigmoid(ga_ref[...].astype(F32)) * y_attn)
    h1 = x_ref[...] + jnp.dot(merged.astype(BF16), wo_ref[...], preferred_element_type=F32)
    h1_ref[...] = h1

    ms = jnp.mean(h1 * h1, axis=-1, keepdims=True)
    n = h1 * lax.rsqrt(ms + RMS_EPS) * gffn_ref[...]
    _store_slabs(n, npk_ref, tm)

    n_hi = n.astype(BF16)
    n_lo = (n - n_hi.astype(F32)).astype(BF16)
    logits = (jnp.dot(n_hi, wrh_ref[...], preferred_element_type=F32)
              + jnp.dot(n_lo, wrh_ref[...], preferred_element_type=F32)
              + jnp.dot(n_hi, wrl_ref[...], preferred_element_type=F32)) + br_ref[...]
    lane = lax.broadcasted_iota(I32, (tm, LANES), 1)
    lg = jnp.where(lane < N_EXPERTS, logits, -jnp.inf)

    sels, tops, idxs = [], [], []
    for _ in range(TOP_K):
        m = jnp.max(lg, axis=-1, keepdims=True)
        idx = jnp.min(jnp.where(lg == m, lane, LANES), axis=-1, keepdims=True)
        sel = lane == idx
        sels.append(sel)
        tops.append(m)
        idxs.append(idx)
        lg = jnp.where(sel, -jnp.inf, lg)
    exps = [jnp.exp(t - tops[0]) for t in tops]
    denom = exps[0] + exps[1] + exps[2] + exps[3]
    wts = [e / denom for e in exps]

    onehot = jnp.zeros((tm, LANES), F32)
    for sel in sels:
        onehot = onehot + sel.astype(F32)
    base = carry_ref[0:1, :] + jnp.dot(ltri_ref[...], onehot.astype(BF16), preferred_element_type=F32)
    meta = jnp.zeros((tm, LANES), I32)
    wlanes = jnp.zeros((tm, LANES), F32)
    for k in range(TOP_K):
        rank = jnp.sum(jnp.where(sels[k], base, 0.0), axis=-1, keepdims=True)
        meta = jnp.where(lane == k, idxs[k], meta)
        meta = jnp.where(lane == TOP_K + k, rank.astype(I32), meta)
        wlanes = jnp.where(lane == k, wts[k], wlanes)
    meta_ref[...] = meta
    wts_ref[...] = wlanes
    carry_ref[0:1, :] = carry_ref[0:1, :] + jnp.sum(onehot, axis=0, keepdims=True)
    cnt_ref[...] = jnp.broadcast_to(carry_ref[0:1, :], cnt_ref.shape).astype(I32)


def _tri_strict_lower(n):
    r = lax.broadcasted_iota(I32, (n, n), 0)
    c = lax.broadcasted_iota(I32, (n, n), 1)
    return (c < r).astype(BF16)


def _const_spec(shape):
    return pl.BlockSpec(shape, lambda i: (0,) * len(shape))


def _mixer(proj, proj_meta, attn_o, x2d, conv_w, wc, wa, wo, g_ffn, wr_hi, wr_lo, b_r, seq, tm):
    n_tok = x2d.shape[0]
    tiles_per_seq = seq // tm
    meta_blk = META_ROWS // (2 * SUBLANES) - 1
    in_specs = [
        pl.BlockSpec((tm, CONV_CH), lambda i: (i, 0)),
        pl.BlockSpec((tm, CONV_CH), lambda i: (i, 1)),
        pl.BlockSpec((tm, CONV_CH), lambda i: (i, 2)),
        pl.BlockSpec((tm, D_MODEL), lambda i: (i, 3)),
        pl.BlockSpec((tm, D_MODEL), lambda i: (i, 4)),
        pl.BlockSpec((tm, ATTN_WIDTH), lambda i: (i, 0)),
        pl.BlockSpec((tm, D_MODEL), lambda i: (i, 0)),
        pl.BlockSpec((2 * SUBLANES, CONV_CH), lambda i: (meta_blk, 0)),
        pl.BlockSpec((2 * SUBLANES, CONV_CH), lambda i: (meta_blk, 2)),
        _const_spec((SUBLANES, CONV_CH)),
        _const_spec((CONV_CH, D_MODEL)),
        _const_spec((ATTN_WIDTH, D_MODEL)),
        _const_spec((D_MODEL, D_MODEL)),
        _const_spec((1, D_MODEL)),
        _const_spec((D_MODEL, LANES)),
        _const_spec((D_MODEL, LANES)),
        _const_spec((1, LANES)),
        _const_spec((tm, tm)),
    ]
    out_specs = [
        pl.BlockSpec((tm, D_MODEL), lambda i: (i, 0)),
        pl.BlockSpec(_tok_shape(tm), lambda i: (0, i, 0)),
        pl.BlockSpec((tm, LANES), lambda i: (i, 0)),
        pl.BlockSpec((tm, LANES), lambda i: (i, 0)),
        _const_spec((SUBLANES, LANES)),
    ]
    out_shape = [
        jax.ShapeDtypeStruct((n_tok, D_MODEL), F32),
        jax.ShapeDtypeStruct(_tok_shape(n_tok), F32),
        jax.ShapeDtypeStruct((n_tok, LANES), I32),
        jax.ShapeDtypeStruct((n_tok, LANES), F32),
        jax.ShapeDtypeStruct((SUBLANES, LANES), I32),
    ]
    conv_w8 = jnp.pad(conv_w, ((0, SUBLANES - CONV_K), (0, 0)))
    return pl.pallas_call(
        functools.partial(_mixer_kernel, tm=tm, tiles_per_seq=tiles_per_seq),
        grid=(n_tok // tm,),
        in_specs=in_specs,
        out_specs=out_specs,
        out_shape=out_shape,
        scratch_shapes=[pltpu.VMEM((tm + SUBLANES, CONV_CH), F32), pltpu.VMEM((SUBLANES, LANES), F32)],
        compiler_params=_cparams(("arbitrary",)),
        name="mixer_out",
    )(proj, proj, proj, proj, proj, attn_o, x2d, proj_meta, proj_meta, conv_w8, wc, wa, wo,
      g_ffn.reshape(1, D_MODEL), wr_hi, wr_lo, b_r, _tri_strict_lower(tm))


def _row_slab(ref, row):
    return ref.at[:, pl.ds(pl.multiple_of(row * TOK_ROWS, TOK_ROWS), TOK_ROWS), :]


def _wait_rows(hbm_ref, vmem_or_hbm_ref, sem, n_rows):
    n_sub = n_rows * TOK_ROWS
    pltpu.make_async_copy(hbm_ref.at[:, pl.ds(0, n_sub), :], vmem_or_hbm_ref.at[:, pl.ds(0, n_sub), :],
                          sem).wait()


def _dispatch_kernel(eid_ref, rank_ref, offs_ref, npk_ref, xg_in_ref, xg_ref, sem, *, tm):
    del xg_in_ref

    def issue(t, c):
        for k in range(TOP_K):
            j = t * TOP_K + k
            dst = offs_ref[eid_ref[j]] + rank_ref[j]
            pltpu.make_async_copy(_row_slab(npk_ref, t), _row_slab(xg_ref, dst), sem).start(priority=k % 2)
        return c

    lax.fori_loop(0, tm, issue, 0, unroll=4)
    _wait_rows(xg_ref, xg_ref, sem, tm * TOP_K)


def _dispatch(eid_flat, rank_flat, offs, npk, xg0, tm):
    n_tok = npk.shape[1] // TOK_ROWS
    n_rows = xg0.shape[1] // TOK_ROWS
    smem_blk = pl.BlockSpec((tm * TOP_K,), lambda i: (i,), memory_space=pltpu.SMEM)
    return pl.pallas_call(
        functools.partial(_dispatch_kernel, tm=tm),
        grid=(n_tok // tm,),
        in_specs=[smem_blk, smem_blk,
                  pl.BlockSpec(memory_space=pltpu.SMEM),
                  pl.BlockSpec(_tok_shape(tm), lambda i: (0, i, 0)),
                  pl.BlockSpec(memory_space=pl.ANY)],
        out_specs=pl.BlockSpec(memory_space=pl.ANY),
        out_shape=jax.ShapeDtypeStruct(_tok_shape(n_rows), F32),
        scratch_shapes=[pltpu.SemaphoreType.DMA(())],
        input_output_aliases={4: 0},
        compiler_params=_cparams(("arbitrary",)),
        name="moe_dispatch",
    )(eid_flat, rank_flat, offs, npk, xg0)


def _expert_kernel(te_ref, tb_ref, tv_ref, nu_ref, x_ref, wg_ref, bg_ref, wu_ref, bu_ref, wd_ref, bd_ref,
                   o_ref, xs_ref, acc_ref, *, n_fc, tr, sub):
    del te_ref, tb_ref
    t = pl.program_id(0)
    f = pl.program_id(1)

    def tile_body(rows):
        @pl.when(f == 0)
        def _():
            for c in range(N_CHUNKS):
                xs_ref[0:rows, c * LANES:(c + 1) * LANES] = x_ref[_slab_idx(0, rows, c)].astype(BF16)
            acc_ref[0:rows, :] = jnp.zeros((rows, acc_ref.shape[1]), F32)

        x = xs_ref[0:rows, :]
        gate = jnp.dot(x, wg_ref[0, 0], preferred_element_type=F32) + bg_ref[0]
        up = jnp.dot(x, wu_ref[0, 0], preferred_element_type=F32) + bu_ref[0]
        gate = jnp.minimum(gate, SWIGLU_LIMIT)
        up = jnp.clip(up, -SWIGLU_LIMIT, SWIGLU_LIMIT)
        act = (up + 1.0) * (gate * jax.nn.sigmoid(SWIGLU_ALPHA * gate))
        acc_ref[0:rows, :] += jnp.dot(act.astype(BF16), wd_ref[0], preferred_element_type=F32)

        @pl.when(f == n_fc - 1)
        def _():
            _store_slabs(acc_ref[0:rows, :] + bd_ref[0], o_ref, rows)
            if rows < tr:
                o_ref[:, rows * TOK_ROWS:tr * TOK_ROWS, :] = jnp.zeros(
                    (TOK_HALVES, (tr - rows) * TOK_ROWS, LANES), F32)

    n_sub = (tv_ref[t] + sub - 1) // sub
    for nb in range(1, tr // sub + 1):
        pl.when(jnp.logical_and(t < nu_ref[0], n_sub == nb))(functools.partial(tile_body, nb * sub))


def _experts(tile_expert, tile_block, tile_valid, n_used, xg, w_gate, b_gate, w_up, b_up, w_down, b_down,
             tr, sub):
    n_rows = xg.shape[1] // TOK_ROWS
    n_exp, n_fc, d, fc = w_gate.shape
    dff = n_fc * fc
    grid_spec = pltpu.PrefetchScalarGridSpec(
        num_scalar_prefetch=4,
        grid=(n_rows // tr, n_fc),
        in_specs=[
            pl.BlockSpec(_tok_shape(tr), lambda t, f, te, tb, tv, nu: (0, tb[t], 0)),
            pl.BlockSpec((1, 1, d, fc), lambda t, f, te, tb, tv, nu: (te[t], f, 0, 0)),
            pl.BlockSpec((1, 1, fc), lambda t, f, te, tb, tv, nu: (te[t], 0, f)),
            pl.BlockSpec((1, 1, d, fc), lambda t, f, te, tb, tv, nu: (te[t], f, 0, 0)),
            pl.BlockSpec((1, 1, fc), lambda t, f, te, tb, tv, nu: (te[t], 0, f)),
            pl.BlockSpec((1, fc, d), lambda t, f, te, tb, tv, nu: (te[t], f, 0)),
            pl.BlockSpec((1, 1, d), lambda t, f, te, tb, tv, nu: (te[t], 0, 0)),
        ],
        out_specs=pl.BlockSpec(_tok_shape(tr), lambda t, f, te, tb, tv, nu: (0, tb[t], 0),
                               pipeline_mode=pl.Buffered(1)),
        scratch_shapes=[pltpu.VMEM((tr, d), BF16), pltpu.VMEM((tr, d), F32)],
    )
    return pl.pallas_call(
        functools.partial(_expert_kernel, n_fc=n_fc, tr=tr, sub=sub),
        grid_spec=grid_spec,
        out_shape=jax.ShapeDtypeStruct(_tok_shape(n_rows), F32),
        input_output_aliases={4: 0},
        compiler_params=_cparams(("arbitrary", "arbitrary"), EXPERT_VMEM_LIMIT),
        name="moe_experts",
    )(tile_expert, tile_block, tile_valid, n_used, xg, w_gate, b_gate.reshape(n_exp, 1, dff), w_up,
      b_up.reshape(n_exp, 1, dff), w_down, b_down.reshape(n_exp, 1, d))


def _combine_kernel(eid_ref, rank_ref, eid_next_ref, rank_next_ref, offs_ref, og_ref, wts_ref, h1_ref,
                    gfin_ref, out_ref, gbuf_ref, h2_ref, sems, *, tm):
    i = pl.program_id(0)
    slot = i % 2

    def gather(e_ref, r_ref, dst_slot):
        gdst = gbuf_ref.at[dst_slot]

        def issue(t, c):
            for k in range(TOP_K):
                j = t * TOP_K + k
                src = offs_ref[e_ref[j]] + r_ref[j]
                pltpu.make_async_copy(_row_slab(og_ref, src), _row_slab(gdst, k * tm + t),
                                      sems.at[dst_slot]).start(priority=k % 2)
            return c

        lax.fori_loop(0, tm, issue, 0, unroll=4)

    @pl.when(i == 0)
    def _():
        gather(eid_ref, rank_ref, 0)

    @pl.when(i + 1 < pl.num_programs(0))
    def _():
        gather(eid_next_ref, rank_next_ref, 1 - slot)

    gcur = gbuf_ref.at[slot]
    _wait_rows(og_ref, gcur, sems.at[slot], tm * TOP_K)

    wts = wts_ref[...]
    wk = [wts[:, k:k + 1] for k in range(TOP_K)]
    for c in range(N_CHUNKS):
        y = h1_ref[:, c * LANES:(c + 1) * LANES]
        for k in range(TOP_K):
            y = y + wk[k] * gcur[_slab_idx(k * tm, tm, c)]
        h2_ref[:, c * LANES:(c + 1) * LANES] = y
    h2 = h2_ref[...]
    ms = jnp.mean(h2 * h2, axis=-1, keepdims=True)
    out_ref[...] = h2 * lax.rsqrt(ms + RMS_EPS) * gfin_ref[...]


def _combine(eid_flat, rank_flat, offs, og, wts, h1, g_final, tm):
    n_tok = h1.shape[0]
    n_steps = n_tok // tm
    smem_blk = pl.BlockSpec((tm * TOP_K,), lambda i: (i,), memory_space=pltpu.SMEM)
    smem_next = pl.BlockSpec((tm * TOP_K,), lambda i: (jnp.minimum(i + 1, n_steps - 1),),
                             memory_space=pltpu.SMEM)
    return pl.pallas_call(
        functools.partial(_combine_kernel, tm=tm),
        grid=(n_steps,),
        in_specs=[smem_blk, smem_blk, smem_next, smem_next,
                  pl.BlockSpec(memory_space=pltpu.SMEM),
                  pl.BlockSpec(memory_space=pl.ANY),
                  pl.BlockSpec((tm, LANES), lambda i: (i, 0)),
                  pl.BlockSpec((tm, D_MODEL), lambda i: (i, 0)),
                  pl.BlockSpec((1, D_MODEL), lambda i: (0, 0))],
        out_specs=pl.BlockSpec((tm, D_MODEL), lambda i: (i, 0)),
        out_shape=jax.ShapeDtypeStruct((n_tok, D_MODEL), F32),
        scratch_shapes=[pltpu.VMEM((2,) + _tok_shape(tm * TOP_K), F32),
                        pltpu.VMEM((tm, D_MODEL), F32),
                        pltpu.SemaphoreType.DMA((2,))],
        compiler_params=_cparams(("arbitrary",)),
        name="moe_combine",
    )(eid_flat, rank_flat, eid_flat, rank_flat, offs, og, wts, h1, g_final.reshape(1, D_MODEL))


def _routing_tables(counts, tr, n_tiles):
    ntile = (counts + tr - 1) // tr
    tiles_cum = jnp.cumsum(ntile)
    offs = ((tiles_cum - ntile) * tr).astype(I32)
    n_used = tiles_cum[-1]
    t = jnp.minimum(jnp.arange(n_tiles, dtype=I32), n_used - 1)
    tile_expert = jnp.sum((tiles_cum[None, :] <= t[:, None]).astype(I32), axis=1)
    onehot = (tile_expert[:, None] == jnp.arange(counts.shape[0], dtype=I32)[None, :]).astype(I32)
    expert_end = jnp.sum(onehot * (offs + counts)[None, :], axis=1)
    tile_valid = jnp.clip(expert_end - t * tr, 1, tr).astype(I32)
    return offs, tile_expert, t.astype(I32), tile_valid, n_used.reshape(1).astype(I32)


def kernel(x, meta_tokens, g_mix, w_in, conv_w, w_conv_out, w_attn_out, w_o, g_ffn, w_router,
           b_router, w_gate, b_gate, w_up, b_up, w_down, b_down, g_final):
    assert g_mix.shape[0] == 1, "single-layer trunk"
    bsz, seq, d = x.shape
    n_tok = bsz * seq
    x2d = x.reshape(n_tok, d)

    n_tiles = (n_tok * TOP_K) // EXPERT_TR + N_EXPERTS
    hn = _rmsnorm_bf16(x2d, g_mix[0], RMS_TM)
    proj, xg0 = _in_proj(hn, w_in[0], PROJ_TM, PROJ_TN, zero_rows=n_tiles * EXPERT_TR)
    meta_pad = jnp.pad(meta_tokens.astype(x.dtype), ((META_ROWS - N_META, 0), (0, 0)))
    hn_meta = _rmsnorm_bf16(meta_pad, g_mix[0], META_ROWS)
    proj_meta = _in_proj(hn_meta, w_in[0], META_ROWS, PROJ_TN)

    attn_o, wg_bf, wu_bf, wd_bf = _attention(proj, proj_meta, w_gate[0], w_up[0], w_down[0],
                                             bsz, seq, ATT_TQ, ATT_TK, EXPERT_FC)

    wr = jnp.pad(w_router[0], ((0, 0), (0, LANES - N_EXPERTS)))
    wr_hi = wr.astype(BF16)
    wr_lo = (wr - wr_hi.astype(F32)).astype(BF16)
    b_r = jnp.pad(b_router[0], (0, LANES - N_EXPERTS)).reshape(1, LANES)
    h1, npk, meta, wts, cnt = _mixer(proj, proj_meta, attn_o, x2d, conv_w[0],
                                     w_conv_out[0].astype(BF16), w_attn_out[0].astype(BF16),
                                     w_o[0].astype(BF16), g_ffn[0], wr_hi, wr_lo, b_r, seq, MIX_TM)

    offs, tile_expert, tile_block, tile_valid, n_used = _routing_tables(cnt[0, :N_EXPERTS], EXPERT_TR, n_tiles)
    eid_flat = meta[:, 0:TOP_K].reshape(-1)
    rank_flat = meta[:, TOP_K:2 * TOP_K].reshape(-1)

    xg = _dispatch(eid_flat, rank_flat, offs, npk, xg0, DISPATCH_TM)
    og = _experts(tile_expert, tile_block, tile_valid, n_used, xg, wg_bf, b_gate[0], wu_bf, b_up[0],
                  wd_bf, b_down[0], EXPERT_TR, EXPERT_SUB)
    out = _combine(eid_flat, rank_flat, offs, og, wts, h1, g_final, COMBINE_TM)
    return out.reshape(bsz, seq, d)
```

```python
import functools
import math

import jax
import jax.numpy as jnp
from jax import lax
from jax.experimental import pallas as pl
from jax.experimental.pallas import tpu as pltpu

F32 = jnp.float32
BF16 = jnp.bfloat16
I32 = jnp.int32

D_MODEL = 2048
N_META = 16
N_HEADS = 8
HEAD_DIM = 128
ATTN_WIDTH = N_HEADS * HEAD_DIM
CONV_CH = D_MODEL // 2
CONV_K = 3
N_EXPERTS = 32
TOP_K = 4
D_FF = D_MODEL
SWIGLU_LIMIT = 7.0
SWIGLU_ALPHA = 1.702
RMS_EPS = 1e-5
IN_WIDTH = 3 * CONV_CH + 3 * ATTN_WIDTH + 2 * D_MODEL

LANES = 128
SUBLANES = 8
META_ROWS = 128
LOG2E = 1.4426950408889634
VMEM_LIMIT = 56 * 1024 * 1024
RMS_TM = 512
PROJ_TM = 1024
PROJ_TN = 1024
ATT_TQ = 512
ATT_TK = 256
MIX_TM = 256
DISPATCH_TM = 256
EXPERT_TR = 512
EXPERT_FC = 1024
EXPERT_SUB = 256
COMBINE_TM = 256
TOK_ROWS = SUBLANES
TOK_HALVES = D_MODEL // (TOK_ROWS * LANES)


def _tok_shape(n_rows):
    return (TOK_HALVES, n_rows * TOK_ROWS, LANES)


def _cparams(sem, vmem=VMEM_LIMIT):
    return pltpu.CompilerParams(dimension_semantics=sem, vmem_limit_bytes=vmem)


def _rmsnorm_kernel(x_ref, g_ref, o_ref):
    x = x_ref[...].astype(F32)
    ms = jnp.mean(x * x, axis=-1, keepdims=True)
    o_ref[...] = (x * lax.rsqrt(ms + RMS_EPS) * g_ref[...]).astype(o_ref.dtype)


def _rmsnorm_bf16(x, g, tm):
    m, d = x.shape
    return pl.pallas_call(
        _rmsnorm_kernel,
        grid=(m // tm,),
        in_specs=[pl.BlockSpec((tm, d), lambda i: (i, 0)),
                  pl.BlockSpec((1, d), lambda i: (0, 0))],
        out_specs=pl.BlockSpec((tm, d), lambda i: (i, 0)),
        out_shape=jax.ShapeDtypeStruct((m, d), BF16),
        compiler_params=_cparams(("arbitrary",)),
        name="rmsnorm",
    )(x, g.reshape(1, d))


ZSCALE = LOG2E / math.sqrt(HEAD_DIM)
Q_COL0 = 3 * CONV_CH


def _in_proj_kernel(x_ref, w_ref, o_ref, *rest, q_tile, zero_fill):
    wbf_ref = rest[-1]
    j = pl.program_id(0)

    @pl.when(pl.program_id(1) == 0)
    def _():
        wbf_ref[...] = w_ref[...].astype(BF16)

    acc = jnp.dot(x_ref[...], wbf_ref[...], preferred_element_type=F32)
    o_ref[...] = (acc * jnp.where(j == q_tile, ZSCALE, 1.0)).astype(o_ref.dtype)
    if zero_fill:
        rest[0][...] = jnp.zeros_like(rest[0])


def _in_proj(x, w, tm, tn, zero_rows=0):
    m, k = x.shape
    _, n = w.shape
    assert Q_COL0 % tn == 0 and ATTN_WIDTH == tn
    steps = (n // tn) * (m // tm)
    in_specs = [pl.BlockSpec((tm, k), lambda j, i: (i, 0)),
                pl.BlockSpec((k, tn), lambda j, i: (0, j))]
    out_specs = [pl.BlockSpec((tm, tn), lambda j, i: (i, j))]
    out_shape = [jax.ShapeDtypeStruct((m, n), BF16)]
    if zero_rows:
        nblk = max(d for d in range(1, steps + 1) if zero_rows % d == 0)
        n_i = m // tm
        out_specs.append(pl.BlockSpec(_tok_shape(zero_rows // nblk),
                                      lambda j, i: (0, jnp.minimum(j * n_i + i, nblk - 1), 0)))
        out_shape.append(jax.ShapeDtypeStruct(_tok_shape(zero_rows), F32))
    out = pl.pallas_call(
        functools.partial(_in_proj_kernel, q_tile=Q_COL0 // tn, zero_fill=bool(zero_rows)),
        grid=(n // tn, m // tm),
        in_specs=in_specs,
        out_specs=out_specs,
        out_shape=out_shape,
        scratch_shapes=[pltpu.VMEM((k, tn), BF16)],
        compiler_params=_cparams(("arbitrary", "arbitrary")),
        name="in_proj",
    )(x, w)
    return out if zero_rows else out[0]


def _attn_kernel(q_ref, k_ref, v_ref, km_ref, vm_ref, u_ref, um_ref, wg_ref, wu_ref, wd_ref,
                 o_ref, wgo_ref, wuo_ref, wdo_ref, vt_ref, vmt_ref, acc_ref, r_ref,
                 *, seq, tq, tk, fc):
    for c in range(D_FF // fc):
        wgo_ref[0, c] = wg_ref[:, c * fc:(c + 1) * fc].astype(BF16)
        wuo_ref[0, c] = wu_ref[:, c * fc:(c + 1) * fc].astype(BF16)
    wdo_ref[...] = wd_ref[...].astype(BF16)

    p = pl.program_id(2)
    n_q = seq // tq
    n_sub = tq // tk

    @pl.when(p == 0)
    def _():
        for j in range(seq // tk):
            vt_ref[j] = v_ref[0, j * tk:(j + 1) * tk, :].astype(F32).T.astype(BF16)
        vmt_ref[...] = vm_ref[...].astype(F32).T.astype(BF16)

    u = u_ref[...]
    krow = lax.broadcasted_iota(I32, (tk, tq), 0)
    qcol = lax.broadcasted_iota(I32, (tk, tq), 1)
    valid_meta = lax.broadcasted_iota(I32, (META_ROWS, tq), 0) >= META_ROWS - N_META

    def q_tile(qi):
        q0 = pl.multiple_of(qi * tq, tq)
        q = q_ref[0, pl.ds(q0, tq), :]
        acc_ref[...] = jnp.zeros_like(acc_ref)
        r_ref[...] = jnp.zeros_like(r_ref)

        def sweep(tiles):
            z2s = [lax.dot_general(k, q, (((1,), (1,)), ((), ())), preferred_element_type=F32)
                   for k, _, _, _ in tiles]
            cums = []
            for z2, (_, _, ut, mask) in zip(z2s, tiles):
                e = jnp.exp2(-jnp.abs(z2))
                s2 = jnp.maximum(z2, 0.0) + jnp.log2(1.0 + e)
                if mask is not None:
                    s2 = jnp.where(mask, s2, 0.0)
                cums.append(jnp.dot(ut, s2.astype(BF16), preferred_element_type=F32))
            r = r_ref[...]
            pv = None
            for z2, cum, (_, vt, _, mask) in zip(z2s, cums, tiles):
                w = jnp.exp2(z2 - cum - r)
                if mask is not None:
                    w = jnp.where(mask, w, 0.0)
                part = jnp.dot(vt, w.astype(BF16), preferred_element_type=F32)
                pv = part if pv is None else pv + part
                r = r + cum[0:1, :]
            acc_ref[...] += pv
            r_ref[...] = r

        def key_tile(j, mask):
            k0 = pl.multiple_of(j * tk, tk)
            return (k_ref[0, pl.ds(k0, tk), :], vt_ref[j], u, mask)

        sweep([key_tile(qi * n_sub + d, krow + d * tk < qcol) for d in reversed(range(n_sub))])

        def full(jj, c):
            base = (qi - 2 - 2 * jj) * n_sub
            sweep([key_tile(base + d, None) for d in reversed(range(2 * n_sub))])
            return c

        lax.fori_loop(0, qi // 2, full, 0)

        @pl.when(qi % 2 == 1)
        def _():
            sweep([key_tile(d, None) for d in reversed(range(n_sub))])

        sweep([(km_ref[...], vmt_ref[...], um_ref[...], valid_meta)])
        o_ref[0, pl.ds(q0, tq), :] = acc_ref[...].T.astype(o_ref.dtype)

    q_tile(p)
    q_tile(n_q - 1 - p)


def _tri_upper_incl(n):
    r = lax.broadcasted_iota(I32, (n, n), 0)
    c = lax.broadcasted_iota(I32, (n, n), 1)
    return (c >= r).astype(BF16)


def _attention(proj, proj_meta, w_gate, w_up, w_down, bsz, seq, tq, tk, fc):
    proj3 = proj.reshape(bsz, seq, IN_WIDTH)
    qb = Q_COL0 // HEAD_DIM
    kb, vb = qb + N_HEADS, qb + 2 * N_HEADS
    n_pair = seq // tq // 2
    n_exp, d, dff = w_gate.shape
    n_fc = dff // fc
    steps = bsz * N_HEADS * n_pair
    rc = (n_exp * d) // steps
    assert dff == d and rc * steps == n_exp * d and d % rc == 0 and rc % (2 * SUBLANES) == 0
    blk_per_e = d // rc

    def step(b, h, p):
        return (b * N_HEADS + h) * n_pair + p

    seq_spec = lambda col0: pl.BlockSpec((1, seq, HEAD_DIM), lambda b, h, p: (b, 0, col0 + h))
    w_in_spec = pl.BlockSpec((rc, dff), lambda b, h, p: (step(b, h, p), 0))
    wgu_out_spec = pl.BlockSpec((1, n_fc, rc, fc),
                                lambda b, h, p: (step(b, h, p) // blk_per_e, 0, step(b, h, p) % blk_per_e, 0))
    out, wg_bf, wu_bf, wd_bf = pl.pallas_call(
        functools.partial(_attn_kernel, seq=seq, tq=tq, tk=tk, fc=fc),
        grid=(bsz, N_HEADS, n_pair),
        in_specs=[seq_spec(qb), seq_spec(kb), seq_spec(vb),
                  pl.BlockSpec((META_ROWS, HEAD_DIM), lambda b, h, p: (0, kb + h)),
                  pl.BlockSpec((META_ROWS, HEAD_DIM), lambda b, h, p: (0, vb + h)),
                  pl.BlockSpec((tk, tk), lambda b, h, p: (0, 0)),
                  pl.BlockSpec((META_ROWS, META_ROWS), lambda b, h, p: (0, 0)),
                  w_in_spec, w_in_spec,
                  pl.BlockSpec((rc, d), lambda b, h, p: (step(b, h, p), 0))],
        out_specs=[pl.BlockSpec((1, seq, HEAD_DIM), lambda b, h, p: (b, 0, h)),
                   wgu_out_spec, wgu_out_spec,
                   pl.BlockSpec((rc, d), lambda b, h, p: (step(b, h, p), 0))],
        out_shape=[jax.ShapeDtypeStruct((bsz, seq, ATTN_WIDTH), BF16),
                   jax.ShapeDtypeStruct((n_exp, n_fc, d, fc), BF16),
                   jax.ShapeDtypeStruct((n_exp, n_fc, d, fc), BF16),
                   jax.ShapeDtypeStruct((n_exp * dff, d), BF16)],
        scratch_shapes=[pltpu.VMEM((seq // tk, HEAD_DIM, tk), BF16),
                        pltpu.VMEM((HEAD_DIM, META_ROWS), BF16),
                        pltpu.VMEM((HEAD_DIM, tq), F32),
                        pltpu.VMEM((1, tq), F32)],
        compiler_params=_cparams(("arbitrary", "arbitrary", "arbitrary")),
        name="stickbreak_attn",
    )(proj3, proj3, proj3, proj_meta, proj_meta, _tri_upper_incl(tk), _tri_upper_incl(META_ROWS),
      w_gate.reshape(n_exp * d, dff), w_up.reshape(n_exp * d, dff), w_down.reshape(n_exp * dff, d))
    return out.reshape(bsz * seq, ATTN_WIDTH), wg_bf, wu_bf, wd_bf.reshape(n_exp, dff, d)


def _slab_idx(first_tok, n_tok, c):
    h, s = divmod(c, TOK_ROWS)
    return (h, pl.ds(first_tok * TOK_ROWS + s, n_tok, stride=TOK_ROWS), slice(None))


N_CHUNKS = D_MODEL // LANES


def _store_slabs(vals, out_ref, n_tok):
    for c in range(N_CHUNKS):
        out_ref[_slab_idx(0, n_tok, c)] = vals[:, c * LANES:(c + 1) * LANES]


def _mixer_kernel(u_ref, bp_ref, cp_ref, gc_ref, ga_ref, o_ref, x_ref, um_ref, cm_ref,
                  convw_ref, wc_ref, wa_ref, wo_ref, gffn_ref, wrh_ref, wrl_ref, br_ref, ltri_ref,
                  h1_ref, npk_ref, meta_ref, wts_ref, cnt_ref,
                  cu_ref, carry_ref, *, tm, tiles_per_seq):
    i = pl.program_id(0)
    first = (i % tiles_per_seq) == 0

    @pl.when(i == 0)
    def _():
        carry_ref[...] = jnp.zeros_like(carry_ref)

    @pl.when(first)
    def _():
        cum = cm_ref[...].astype(F32) * um_ref[...].astype(F32)
        cu_ref[0:SUBLANES, :] = cum[SUBLANES:2 * SUBLANES, :]

    @pl.when(jnp.logical_not(first))
    def _():
        cu_ref[0:SUBLANES, :] = cu_ref[tm:tm + SUBLANES, :]

    cu = cp_ref[...].astype(F32) * u_ref[...].astype(F32)
    cu_ref[SUBLANES:tm + SUBLANES, :] = cu
    cw = convw_ref[...]
    conv = (cu_ref[SUBLANES - 2:tm + SUBLANES - 2, :] * cw[0:1, :]
            + cu_ref[SUBLANES - 1:tm + SUBLANES - 1, :] * cw[1:2, :]
            + cu * cw[2:3, :])
    y_conv = jnp.dot((bp_ref[...].astype(F32) * conv).astype(BF16), wc_ref[...],
                     preferred_element_type=F32)
    y_attn = jnp.dot(o_ref[...], wa_ref[...], preferred_element_type=F32)
    merged = (jax.nn.sigmoid(gc_ref[...].astype(F32)) * y_conv
              + jax.nn.sigmoid(ga_ref[...].astype(F32)) * y_attn)
    h1 = x_ref[...] + jnp.dot(merged.astype(BF16), wo_ref[...], preferred_element_type=F32)
    h1_ref[...] = h1

    ms = jnp.mean(h1 * h1, axis=-1, keepdims=True)
    n = h1 * lax.rsqrt(ms + RMS_EPS) * gffn_ref[...]
    _store_slabs(n, npk_ref, tm)

    n_hi = n.astype(BF16)
    n_lo = (n - n_hi.astype(F32)).astype(BF16)
    logits = (jnp.dot(n_hi, wrh_ref[...], preferred_element_type=F32)
              + jnp.dot(n_lo, wrh_ref[...], preferred_element_type=F32)
              + jnp.dot(n_hi, wrl_ref[...], preferred_element_type=F32)) + br_ref[...]
    lane = lax.broadcasted_iota(I32, (tm, LANES), 1)
    lg = jnp.where(lane < N_EXPERTS, logits, -jnp.inf)

    sels, tops, idxs = [], [], []
    for _ in range(TOP_K):
        m = jnp.max(lg, axis=-1, keepdims=True)
        idx = jnp.min(jnp.where(lg == m, lane, LANES), axis=-1, keepdims=True)
        sel = lane == idx
        sels.append(sel)
        tops.append(m)
        idxs.append(idx)
        lg = jnp.where(sel, -jnp.inf, lg)
    exps = [jnp.exp(t - tops[0]) for t in tops]
    denom = exps[0] + exps[1] + exps[2] + exps[3]
    wts = [e / denom for e in exps]

    onehot = jnp.zeros((tm, LANES), F32)
    for sel in sels:
        onehot = onehot + sel.astype(F32)
    base = carry_ref[0:1, :] + jnp.dot(ltri_ref[...], onehot.astype(BF16), preferred_element_type=F32)
    meta = jnp.zeros((tm, LANES), I32)
    wlanes = jnp.zeros((tm, LANES), F32)
    for k in range(TOP_K):
        rank = jnp.sum(jnp.where(sels[k], base, 0.0), axis=-1, keepdims=True)
        meta = jnp.where(lane == k, idxs[k], meta)
        meta = jnp.where(lane == TOP_K + k, rank.astype(I32), meta)
        wlanes = jnp.where(lane == k, wts[k], wlanes)
    meta_ref[...] = meta
    wts_ref[...] = wlanes
    carry_ref[0:1, :] = carry_ref[0:1, :] + jnp.sum(onehot, axis=0, keepdims=True)
    cnt_ref[...] = jnp.broadcast_to(carry_ref[0:1, :], cnt_ref.shape).astype(I32)


def _tri_strict_lower(n):
    r = lax.broadcasted_iota(I32, (n, n), 0)
    c = lax.broadcasted_iota(I32, (n, n), 1)
    return (c < r).astype(BF16)


def _const_spec(shape):
    return pl.BlockSpec(shape, lambda i: (0,) * len(shape))


def _mixer(proj, proj_meta, attn_o, x2d, conv_w, wc, wa, wo, g_ffn, wr_hi, wr_lo, b_r, seq, tm):
    n_tok = x2d.shape[0]
    tiles_per_seq = seq // tm
    meta_blk = META_ROWS // (2 * SUBLANES) - 1
    in_specs = [
        pl.BlockSpec((tm, CONV_CH), lambda i: (i, 0)),
        pl.BlockSpec((tm, CONV_CH), lambda i: (i, 1)),
        pl.BlockSpec((tm, CONV_CH), lambda i: (i, 2)),
        pl.BlockSpec((tm, D_MODEL), lambda i: (i, 3)),
        pl.BlockSpec((tm, D_MODEL), lambda i: (i, 4)),
        pl.BlockSpec((tm, ATTN_WIDTH), lambda i: (i, 0)),
        pl.BlockSpec((tm, D_MODEL), lambda i: (i, 0)),
        pl.BlockSpec((2 * SUBLANES, CONV_CH), lambda i: (meta_blk, 0)),
        pl.BlockSpec((2 * SUBLANES, CONV_CH), lambda i: (meta_blk, 2)),
        _const_spec((SUBLANES, CONV_CH)),
        _const_spec((CONV_CH, D_MODEL)),
        _const_spec((ATTN_WIDTH, D_MODEL)),
        _const_spec((D_MODEL, D_MODEL)),
        _const_spec((1, D_MODEL)),
        _const_spec((D_MODEL, LANES)),
        _const_spec((D_MODEL, LANES)),
        _const_spec((1, LANES)),
        _const_spec((tm, tm)),
    ]
    out_specs = [
        pl.BlockSpec((tm, D_MODEL), lambda i: (i, 0)),
        pl.BlockSpec(_tok_shape(tm), lambda i: (0, i, 0)),
        pl.BlockSpec((tm, LANES), lambda i: (i, 0)),
        pl.BlockSpec((tm, LANES), lambda i: (i, 0)),
        _const_spec((SUBLANES, LANES)),
    ]
    out_shape = [
        jax.ShapeDtypeStruct((n_tok, D_MODEL), F32),
        jax.ShapeDtypeStruct(_tok_shape(n_tok), F32),
        jax.ShapeDtypeStruct((n_tok, LANES), I32),
        jax.ShapeDtypeStruct((n_tok, LANES), F32),
        jax.ShapeDtypeStruct((SUBLANES, LANES), I32),
    ]
    conv_w8 = jnp.pad(conv_w, ((0, SUBLANES - CONV_K), (0, 0)))
    return pl.pallas_call(
        functools.partial(_mixer_kernel, tm=tm, tiles_per_seq=tiles_per_seq),
        grid=(n_tok // tm,),
        in_specs=in_specs,
        out_specs=out_specs,
        out_shape=out_shape,
        scratch_shapes=[pltpu.VMEM((tm + SUBLANES, CONV_CH), F32), pltpu.VMEM((SUBLANES, LANES), F32)],
        compiler_params=_cparams(("arbitrary",)),
        name="mixer_out",
    )(proj, proj, proj, proj, proj, attn_o, x2d, proj_meta, proj_meta, conv_w8, wc, wa, wo,
      g_ffn.reshape(1, D_MODEL), wr_hi, wr_lo, b_r, _tri_strict_lower(tm))


def _row_slab(ref, row):
    return ref.at[:, pl.ds(pl.multiple_of(row * TOK_ROWS, TOK_ROWS), TOK_ROWS), :]


def _wait_rows(hbm_ref, vmem_or_hbm_ref, sem, n_rows):
    n_sub = n_rows * TOK_ROWS
    pltpu.make_async_copy(hbm_ref.at[:, pl.ds(0, n_sub), :], vmem_or_hbm_ref.at[:, pl.ds(0, n_sub), :],
                          sem).wait()


def _dispatch_kernel(eid_ref, rank_ref, offs_ref, npk_ref, xg_in_ref, xg_ref, sem, *, tm):
    del xg_in_ref

    def issue(t, c):
        for k in range(TOP_K):
            j = t * TOP_K + k
            dst = offs_ref[eid_ref[j]] + rank_ref[j]
            pltpu.make_async_copy(_row_slab(npk_ref, t), _row_slab(xg_ref, dst), sem).start(priority=k % 2)
        return c

    lax.fori_loop(0, tm, issue, 0, unroll=4)
    _wait_rows(xg_ref, xg_ref, sem, tm * TOP_K)


def _dispatch(eid_flat, rank_flat, offs, npk, xg0, tm):
    n_tok = npk.shape[1] // TOK_ROWS
    n_rows = xg0.shape[1] // TOK_ROWS
    smem_blk = pl.BlockSpec((tm * TOP_K,), lambda i: (i,), memory_space=pltpu.SMEM)
    return pl.pallas_call(
        functools.partial(_dispatch_kernel, tm=tm),
        grid=(n_tok // tm,),
        in_specs=[smem_blk, smem_blk,
                  pl.BlockSpec(memory_space=pltpu.SMEM),
                  pl.BlockSpec(_tok_shape(tm), lambda i: (0, i, 0)),
                  pl.BlockSpec(memory_space=pl.ANY)],
        out_specs=pl.BlockSpec(memory_space=pl.ANY),
        out_shape=jax.ShapeDtypeStruct(_tok_shape(n_rows), F32),
        scratch_shapes=[pltpu.SemaphoreType.DMA(())],
        input_output_aliases={4: 0},
        compiler_params=_cparams(("arbitrary",)),
        name="moe_dispatch",
    )(eid_flat, rank_flat, offs, npk, xg0)


def _expert_kernel(te_ref, tb_ref, tv_ref, nu_ref, x_ref, wg_ref, bg_ref, wu_ref, bu_ref, wd_ref, bd_ref,
                   o_ref, acc_ref, *, tr, sub):
    del te_ref, tb_ref
    t = pl.program_id(0)
    f = pl.program_id(1)

    @pl.when(jnp.logical_and(t == 0, f == 0))
    def _():
        acc_ref[...] = jnp.zeros_like(acc_ref)

    def tile_body(rows):
        x = jnp.concatenate([x_ref[_slab_idx(0, rows, c)].astype(BF16) for c in range(N_CHUNKS)], axis=-1)
        gate = jnp.dot(x, wg_ref[0, 0], preferred_element_type=F32) + bg_ref[0]
        up = jnp.dot(x, wu_ref[0, 0], preferred_element_type=F32) + bu_ref[0]
        gate = jnp.minimum(gate, SWIGLU_LIMIT)
        up = jnp.clip(up, -SWIGLU_LIMIT, SWIGLU_LIMIT)
        act = (up + 1.0) * (gate * jax.nn.sigmoid(SWIGLU_ALPHA * gate))
        prev = jnp.where(f == 0, 0.0, acc_ref[0:rows, :])
        acc = prev + jnp.dot(act.astype(BF16), wd_ref[0], preferred_element_type=F32)
        acc_ref[0:rows, :] = acc
        _store_slabs(acc + bd_ref[0], o_ref, rows)
        if rows < tr:
            o_ref[:, rows * TOK_ROWS:tr * TOK_ROWS, :] = jnp.zeros(
                (TOK_HALVES, (tr - rows) * TOK_ROWS, LANES), F32)

    n_sub = (tv_ref[t] + sub - 1) // sub
    for nb in range(1, tr // sub + 1):
        pl.when(jnp.logical_and(t < nu_ref[0], n_sub == nb))(functools.partial(tile_body, nb * sub))


def _experts(tile_expert, tile_block, tile_valid, n_used, xg, w_gate, b_gate, w_up, b_up, w_down, b_down,
             tr, sub):
    n_rows = xg.shape[1] // TOK_ROWS
    n_exp, n_fc, d, fc = w_gate.shape
    dff = n_fc * fc
    grid_spec = pltpu.PrefetchScalarGridSpec(
        num_scalar_prefetch=4,
        grid=(n_rows // tr, n_fc),
        in_specs=[
            pl.BlockSpec(_tok_shape(tr), lambda t, f, te, tb, tv, nu: (0, tb[t], 0)),
            pl.BlockSpec((1, 1, d, fc), lambda t, f, te, tb, tv, nu: (te[t], f, 0, 0)),
            pl.BlockSpec((1, 1, fc), lambda t, f, te, tb, tv, nu: (te[t], 0, f)),
            pl.BlockSpec((1, 1, d, fc), lambda t, f, te, tb, tv, nu: (te[t], f, 0, 0)),
            pl.BlockSpec((1, 1, fc), lambda t, f, te, tb, tv, nu: (te[t], 0, f)),
            pl.BlockSpec((1, fc, d), lambda t, f, te, tb, tv, nu: (te[t], f, 0)),
            pl.BlockSpec((1, 1, d), lambda t, f, te, tb, tv, nu: (te[t], 0, 0)),
        ],
        out_specs=pl.BlockSpec(_tok_shape(tr), lambda t, f, te, tb, tv, nu: (0, tb[t], 0)),
        scratch_shapes=[pltpu.VMEM((tr, d), F32)],
    )
    return pl.pallas_call(
        functools.partial(_expert_kernel, tr=tr, sub=sub),
        grid_spec=grid_spec,
        out_shape=jax.ShapeDtypeStruct(_tok_shape(n_rows), F32),
        input_output_aliases={4: 0},
        compiler_params=_cparams(("arbitrary", "arbitrary")),
        name="moe_experts",
    )(tile_expert, tile_block, tile_valid, n_used, xg, w_gate, b_gate.reshape(n_exp, 1, dff), w_up,
      b_up.reshape(n_exp, 1, dff), w_down, b_down.reshape(n_exp, 1, d))


def _combine_kernel(eid_ref, rank_ref, eid_next_ref, rank_next_ref, offs_ref, og_ref, wts_ref, h1_ref,
                    gfin_ref, out_ref, gbuf_ref, h2_ref, sems, *, tm):
    i = pl.program_id(0)
    slot = i % 2

    def gather(e_ref, r_ref, dst_slot):
        gdst = gbuf_ref.at[dst_slot]

        def issue(t, c):
            for k in range(TOP_K):
                j = t * TOP_K + k
                src = offs_ref[e_ref[j]] + r_ref[j]
                pltpu.make_async_copy(_row_slab(og_ref, src), _row_slab(gdst, k * tm + t),
                                      sems.at[dst_slot]).start(priority=k % 2)
            return c

        lax.fori_loop(0, tm, issue, 0, unroll=4)

    @pl.when(i == 0)
    def _():
        gather(eid_ref, rank_ref, 0)

    @pl.when(i + 1 < pl.num_programs(0))
    def _():
        gather(eid_next_ref, rank_next_ref, 1 - slot)

    gcur = gbuf_ref.at[slot]
    _wait_rows(og_ref, gcur, sems.at[slot], tm * TOP_K)

    wts = wts_ref[...]
    wk = [wts[:, k:k + 1] for k in range(TOP_K)]
    for c in range(N_CHUNKS):
        y = h1_ref[:, c * LANES:(c + 1) * LANES]
        for k in range(TOP_K):
            y = y + wk[k] * gcur[_slab_idx(k * tm, tm, c)]
        h2_ref[:, c * LANES:(c + 1) * LANES] = y
    h2 = h2_ref[...]
    ms = jnp.mean(h2 * h2, axis=-1, keepdims=True)
    out_ref[...] = h2 * lax.rsqrt(ms + RMS_EPS) * gfin_ref[...]


def _combine(eid_flat, rank_flat, offs, og, wts, h1, g_final, tm):
    n_tok = h1.shape[0]
    n_steps = n_tok // tm
    smem_blk = pl.BlockSpec((tm * TOP_K,), lambda i: (i,), memory_space=pltpu.SMEM)
    smem_next = pl.BlockSpec((tm * TOP_K,), lambda i: (jnp.minimum(i + 1, n_steps - 1),),
                             memory_space=pltpu.SMEM)
    return pl.pallas_call(
        functools.partial(_combine_kernel, tm=tm),
        grid=(n_steps,),
        in_specs=[smem_blk, smem_blk, smem_next, smem_next,
                  pl.BlockSpec(memory_space=pltpu.SMEM),
                  pl.BlockSpec(memory_space=pl.ANY),
                  pl.BlockSpec((tm, LANES), lambda i: (i, 0)),
                  pl.BlockSpec((tm, D_MODEL), lambda i: (i, 0)),
                  pl.BlockSpec((1, D_MODEL), lambda i: (0, 0))],
        out_specs=pl.BlockSpec((tm, D_MODEL), lambda i: (i, 0)),
        out_shape=jax.ShapeDtypeStruct((n_tok, D_MODEL), F32),
        scratch_shapes=[pltpu.VMEM((2,) + _tok_shape(tm * TOP_K), F32),
                        pltpu.VMEM((tm, D_MODEL), F32),
                        pltpu.SemaphoreType.DMA((2,))],
        compiler_params=_cparams(("arbitrary",)),
        name="moe_combine",
    )(eid_flat, rank_flat, eid_flat, rank_flat, offs, og, wts, h1, g_final.reshape(1, D_MODEL))


def _routing_tables(counts, tr, n_tiles):
    ntile = (counts + tr - 1) // tr
    tiles_cum = jnp.cumsum(ntile)
    offs = ((tiles_cum - ntile) * tr).astype(I32)
    n_used = tiles_cum[-1]
    t = jnp.minimum(jnp.arange(n_tiles, dtype=I32), n_used - 1)
    tile_expert = jnp.sum((tiles_cum[None, :] <= t[:, None]).astype(I32), axis=1)
    onehot = (tile_expert[:, None] == jnp.arange(counts.shape[0], dtype=I32)[None, :]).astype(I32)
    expert_end = jnp.sum(onehot * (offs + counts)[None, :], axis=1)
    tile_valid = jnp.clip(expert_end - t * tr, 1, tr).astype(I32)
    return offs, tile_expert, t.astype(I32), tile_valid, n_used.reshape(1).astype(I32)


def kernel(x, meta_tokens, g_mix, w_in, conv_w, w_conv_out, w_attn_out, w_o, g_ffn, w_router,
           b_router, w_gate, b_gate, w_up, b_up, w_down, b_down, g_final):
    assert g_mix.shape[0] == 1, "single-layer trunk"
    bsz, seq, d = x.shape
    n_tok = bsz * seq
    x2d = x.reshape(n_tok, d)

    n_tiles = (n_tok * TOP_K) // EXPERT_TR + N_EXPERTS
    hn = _rmsnorm_bf16(x2d, g_mix[0], RMS_TM)
    proj, xg0 = _in_proj(hn, w_in[0], PROJ_TM, PROJ_TN, zero_rows=n_tiles * EXPERT_TR)
    meta_pad = jnp.pad(meta_tokens.astype(x.dtype), ((META_ROWS - N_META, 0), (0, 0)))
    hn_meta = _rmsnorm_bf16(meta_pad, g_mix[0], META_ROWS)
    proj_meta = _in_proj(hn_meta, w_in[0], META_ROWS, PROJ_TN)

    attn_o, wg_bf, wu_bf, wd_bf = _attention(proj, proj_meta, w_gate[0], w_up[0], w_down[0],
                                             bsz, seq, ATT_TQ, ATT_TK, EXPERT_FC)

    wr = jnp.pad(w_router[0], ((0, 0), (0, LANES - N_EXPERTS)))
    wr_hi = wr.astype(BF16)
    wr_lo = (wr - wr_hi.astype(F32)).astype(BF16)
    b_r = jnp.pad(b_router[0], (0, LANES - N_EXPERTS)).reshape(1, LANES)
    h1, npk, meta, wts, cnt = _mixer(proj, proj_meta, attn_o, x2d, conv_w[0],
                                     w_conv_out[0].astype(BF16), w_attn_out[0].astype(BF16),
                                     w_o[0].astype(BF16), g_ffn[0], wr_hi, wr_lo, b_r, seq, MIX_TM)

    offs, tile_expert, tile_block, tile_valid, n_used = _routing_tables(cnt[0, :N_EXPERTS], EXPERT_TR, n_tiles)
    eid_flat = meta[:, 0:TOP_K].reshape(-1)
    rank_flat = meta[:, TOP_K:2 * TOP_K].reshape(-1)

    xg = _dispatch(eid_flat, rank_flat, offs, npk, xg0, DISPATCH_TM)
    og = _experts(tile_expert, tile_block, tile_valid, n_used, xg, wg_bf, b_gate[0], wu_bf, b_up[0],
                  wd_bf, b_down[0], EXPERT_TR, EXPERT_SUB)
    out = _combine(eid_flat, rank_flat, offs, og, wts, h1, g_final, COMBINE_TM)
    return out.reshape(bsz, seq, d)
```

```python
import functools
import math

import jax
import jax.numpy as jnp
from jax import lax
from jax.experimental import pallas as pl
from jax.experimental.pallas import tpu as pltpu

F32 = jnp.float32
BF16 = jnp.bfloat16
I32 = jnp.int32

D_MODEL = 2048
N_META = 16
N_HEADS = 8
HEAD_DIM = 128
ATTN_WIDTH = N_HEADS * HEAD_DIM
CONV_CH = D_MODEL // 2
CONV_K = 3
N_EXPERTS = 32
TOP_K = 4
D_FF = D_MODEL
SWIGLU_LIMIT = 7.0
SWIGLU_ALPHA = 1.702
RMS_EPS = 1e-5
IN_WIDTH = 3 * CONV_CH + 3 * ATTN_WIDTH + 2 * D_MODEL

LANES = 128
SUBLANES = 8
META_ROWS = 128
LOG2E = 1.4426950408889634
VMEM_LIMIT = 56 * 1024 * 1024
RMS_TM = 512
PROJ_TM = 1024
PROJ_TN = 1024
ATT_TQ = 512
ATT_TK = 256
MIX_TM = 256
DISPATCH_TM = 256
EXPERT_TR = 640
EXPERT_FC = 1024
EXPERT_SUB = 320
COMBINE_TM = 256
TOK_ROWS = SUBLANES
TOK_HALVES = D_MODEL // (TOK_ROWS * LANES)


def _tok_shape(n_rows):
    return (TOK_HALVES, n_rows * TOK_ROWS, LANES)


def _cparams(sem, vmem=VMEM_LIMIT):
    return pltpu.CompilerParams(dimension_semantics=sem, vmem_limit_bytes=vmem)


def _rmsnorm_kernel(x_ref, g_ref, o_ref):
    x = x_ref[...].astype(F32)
    ms = jnp.mean(x * x, axis=-1, keepdims=True)
    o_ref[...] = (x * lax.rsqrt(ms + RMS_EPS) * g_ref[...]).astype(o_ref.dtype)


def _rmsnorm_bf16(x, g, tm):
    m, d = x.shape
    return pl.pallas_call(
        _rmsnorm_kernel,
        grid=(m // tm,),
        in_specs=[pl.BlockSpec((tm, d), lambda i: (i, 0)),
                  pl.BlockSpec((1, d), lambda i: (0, 0))],
        out_specs=pl.BlockSpec((tm, d), lambda i: (i, 0)),
        out_shape=jax.ShapeDtypeStruct((m, d), BF16),
        compiler_params=_cparams(("arbitrary",)),
        name="rmsnorm",
    )(x, g.reshape(1, d))


ZSCALE = LOG2E / math.sqrt(HEAD_DIM)
Q_COL0 = 3 * CONV_CH


def _in_proj_kernel(x_ref, w_ref, o_ref, *rest, q_tile, zero_fill):
    wbf_ref = rest[-1]
    j = pl.program_id(0)

    @pl.when(pl.program_id(1) == 0)
    def _():
        wbf_ref[...] = w_ref[...].astype(BF16)

    acc = jnp.dot(x_ref[...], wbf_ref[...], preferred_element_type=F32)
    o_ref[...] = (acc * jnp.where(j == q_tile, ZSCALE, 1.0)).astype(o_ref.dtype)
    if zero_fill:
        rest[0][...] = jnp.zeros_like(rest[0])


def _in_proj(x, w, tm, tn, zero_rows=0):
    m, k = x.shape
    _, n = w.shape
    assert Q_COL0 % tn == 0 and ATTN_WIDTH == tn
    steps = (n // tn) * (m // tm)
    in_specs = [pl.BlockSpec((tm, k), lambda j, i: (i, 0)),
                pl.BlockSpec((k, tn), lambda j, i: (0, j))]
    out_specs = [pl.BlockSpec((tm, tn), lambda j, i: (i, j))]
    out_shape = [jax.ShapeDtypeStruct((m, n), BF16)]
    if zero_rows:
        nblk = max(d for d in range(1, steps + 1) if zero_rows % d == 0)
        n_i = m // tm
        out_specs.append(pl.BlockSpec(_tok_shape(zero_rows // nblk),
                                      lambda j, i: (0, jnp.minimum(j * n_i + i, nblk - 1), 0)))
        out_shape.append(jax.ShapeDtypeStruct(_tok_shape(zero_rows), F32))
    out = pl.pallas_call(
        functools.partial(_in_proj_kernel, q_tile=Q_COL0 // tn, zero_fill=bool(zero_rows)),
        grid=(n // tn, m // tm),
        in_specs=in_specs,
        out_specs=out_specs,
        out_shape=out_shape,
        scratch_shapes=[pltpu.VMEM((k, tn), BF16)],
        compiler_params=_cparams(("arbitrary", "arbitrary")),
        name="in_proj",
    )(x, w)
    return out if zero_rows else out[0]


def _attn_kernel(q_ref, k_ref, v_ref, km_ref, vm_ref, u_ref, um_ref, wg_ref, wu_ref, wd_ref,
                 o_ref, wgo_ref, wuo_ref, wdo_ref, vt_ref, vmt_ref, acc_ref, r_ref,
                 *, seq, tq, tk, fc):
    for c in range(D_FF // fc):
        wgo_ref[0, c] = wg_ref[:, c * fc:(c + 1) * fc].astype(BF16)
        wuo_ref[0, c] = wu_ref[:, c * fc:(c + 1) * fc].astype(BF16)
    wdo_ref[...] = wd_ref[...].astype(BF16)

    p = pl.program_id(2)
    n_q = seq // tq
    n_sub = tq // tk

    @pl.when(p == 0)
    def _():
        for j in range(seq // tk):
            vt_ref[j] = v_ref[0, j * tk:(j + 1) * tk, :].astype(F32).T.astype(BF16)
        vmt_ref[...] = vm_ref[...].astype(F32).T.astype(BF16)

    u = u_ref[...]
    krow = lax.broadcasted_iota(I32, (tk, tq), 0)
    qcol = lax.broadcasted_iota(I32, (tk, tq), 1)
    valid_meta = lax.broadcasted_iota(I32, (META_ROWS, tq), 0) >= META_ROWS - N_META

    def q_tile(qi):
        q0 = pl.multiple_of(qi * tq, tq)
        q = q_ref[0, pl.ds(q0, tq), :]
        acc_ref[...] = jnp.zeros_like(acc_ref)
        r_ref[...] = jnp.zeros_like(r_ref)

        def sweep(tiles):
            z2s = [lax.dot_general(k, q, (((1,), (1,)), ((), ())), preferred_element_type=F32)
                   for k, _, _, _ in tiles]
            cums = []
            for z2, (_, _, ut, mask) in zip(z2s, tiles):
                e = jnp.exp2(-jnp.abs(z2))
                s2 = jnp.maximum(z2, 0.0) + jnp.log2(1.0 + e)
                if mask is not None:
                    s2 = jnp.where(mask, s2, 0.0)
                cums.append(jnp.dot(ut, s2.astype(BF16), preferred_element_type=F32))
            r = r_ref[...]
            pv = None
            for z2, cum, (_, vt, _, mask) in zip(z2s, cums, tiles):
                w = jnp.exp2(z2 - cum - r)
                if mask is not None:
                    w = jnp.where(mask, w, 0.0)
                part = jnp.dot(vt, w.astype(BF16), preferred_element_type=F32)
                pv = part if pv is None else pv + part
                r = r + cum[0:1, :]
            acc_ref[...] += pv
            r_ref[...] = r

        def key_tile(j, mask):
            k0 = pl.multiple_of(j * tk, tk)
            return (k_ref[0, pl.ds(k0, tk), :], vt_ref[j], u, mask)

        sweep([key_tile(qi * n_sub + d, krow + d * tk < qcol) for d in reversed(range(n_sub))])

        def full(jj, c):
            base = (qi - 2 - 2 * jj) * n_sub
            sweep([key_tile(base + d, None) for d in reversed(range(2 * n_sub))])
            return c

        lax.fori_loop(0, qi // 2, full, 0)

        @pl.when(qi % 2 == 1)
        def _():
            sweep([key_tile(d, None) for d in reversed(range(n_sub))])

        sweep([(km_ref[...], vmt_ref[...], um_ref[...], valid_meta)])
        o_ref[0, pl.ds(q0, tq), :] = acc_ref[...].T.astype(o_ref.dtype)

    q_tile(p)
    q_tile(n_q - 1 - p)


def _tri_upper_incl(n):
    r = lax.broadcasted_iota(I32, (n, n), 0)
    c = lax.broadcasted_iota(I32, (n, n), 1)
    return (c >= r).astype(BF16)


def _attention(proj, proj_meta, w_gate, w_up, w_down, bsz, seq, tq, tk, fc):
    proj3 = proj.reshape(bsz, seq, IN_WIDTH)
    qb = Q_COL0 // HEAD_DIM
    kb, vb = qb + N_HEADS, qb + 2 * N_HEADS
    n_pair = seq // tq // 2
    n_exp, d, dff = w_gate.shape
    n_fc = dff // fc
    steps = bsz * N_HEADS * n_pair
    rc = (n_exp * d) // steps
    assert dff == d and rc * steps == n_exp * d and d % rc == 0 and rc % (2 * SUBLANES) == 0
    blk_per_e = d // rc

    def step(b, h, p):
        return (b * N_HEADS + h) * n_pair + p

    seq_spec = lambda col0: pl.BlockSpec((1, seq, HEAD_DIM), lambda b, h, p: (b, 0, col0 + h))
    w_in_spec = pl.BlockSpec((rc, dff), lambda b, h, p: (step(b, h, p), 0))
    wgu_out_spec = pl.BlockSpec((1, n_fc, rc, fc),
                                lambda b, h, p: (step(b, h, p) // blk_per_e, 0, step(b, h, p) % blk_per_e, 0))
    out, wg_bf, wu_bf, wd_bf = pl.pallas_call(
        functools.partial(_attn_kernel, seq=seq, tq=tq, tk=tk, fc=fc),
        grid=(bsz, N_HEADS, n_pair),
        in_specs=[seq_spec(qb), seq_spec(kb), seq_spec(vb),
                  pl.BlockSpec((META_ROWS, HEAD_DIM), lambda b, h, p: (0, kb + h)),
                  pl.BlockSpec((META_ROWS, HEAD_DIM), lambda b, h, p: (0, vb + h)),
                  pl.BlockSpec((tk, tk), lambda b, h, p: (0, 0)),
                  pl.BlockSpec((META_ROWS, META_ROWS), lambda b, h, p: (0, 0)),
                  w_in_spec, w_in_spec,
                  pl.BlockSpec((rc, d), lambda b, h, p: (step(b, h, p), 0))],
        out_specs=[pl.BlockSpec((1, seq, HEAD_DIM), lambda b, h, p: (b, 0, h)),
                   wgu_out_spec, wgu_out_spec,
                   pl.BlockSpec((rc, d), lambda b, h, p: (step(b, h, p), 0))],
        out_shape=[jax.ShapeDtypeStruct((bsz, seq, ATTN_WIDTH), BF16),
                   jax.ShapeDtypeStruct((n_exp, n_fc, d, fc), BF16),
                   jax.ShapeDtypeStruct((n_exp, n_fc, d, fc), BF16),
                   jax.ShapeDtypeStruct((n_exp * dff, d), BF16)],
        scratch_shapes=[pltpu.VMEM((seq // tk, HEAD_DIM, tk), BF16),
                        pltpu.VMEM((HEAD_DIM, META_ROWS), BF16),
                        pltpu.VMEM((HEAD_DIM, tq), F32),
                        pltpu.VMEM((1, tq), F32)],
        compiler_params=_cparams(("arbitrary", "arbitrary", "arbitrary")),
        name="stickbreak_attn",
    )(proj3, proj3, proj3, proj_meta, proj_meta, _tri_upper_incl(tk), _tri_upper_incl(META_ROWS),
      w_gate.reshape(n_exp * d, dff), w_up.reshape(n_exp * d, dff), w_down.reshape(n_exp * dff, d))
    return out.reshape(bsz * seq, ATTN_WIDTH), wg_bf, wu_bf, wd_bf.reshape(n_exp, dff, d)


def _slab_idx(first_tok, n_tok, c):
    h, s = divmod(c, TOK_ROWS)
    return (h, pl.ds(first_tok * TOK_ROWS + s, n_tok, stride=TOK_ROWS), slice(None))


N_CHUNKS = D_MODEL // LANES


def _store_slabs(vals, out_ref, n_tok):
    for c in range(N_CHUNKS):
        out_ref[_slab_idx(0, n_tok, c)] = vals[:, c * LANES:(c + 1) * LANES]


def _mixer_kernel(u_ref, bp_ref, cp_ref, gc_ref, ga_ref, o_ref, x_ref, um_ref, cm_ref,
                  convw_ref, wc_ref, wa_ref, wo_ref, gffn_ref, wrh_ref, wrl_ref, br_ref, ltri_ref,
                  h1_ref, npk_ref, meta_ref, wts_ref, cnt_ref,
                  cu_ref, carry_ref, *, tm, tiles_per_seq):
    i = pl.program_id(0)
    first = (i % tiles_per_seq) == 0

    @pl.when(i == 0)
    def _():
        carry_ref[...] = jnp.zeros_like(carry_ref)

    @pl.when(first)
    def _():
        cum = cm_ref[...].astype(F32) * um_ref[...].astype(F32)
        cu_ref[0:SUBLANES, :] = cum[SUBLANES:2 * SUBLANES, :]

    @pl.when(jnp.logical_not(first))
    def _():
        cu_ref[0:SUBLANES, :] = cu_ref[tm:tm + SUBLANES, :]

    cu = cp_ref[...].astype(F32) * u_ref[...].astype(F32)
    cu_ref[SUBLANES:tm + SUBLANES, :] = cu
    cw = convw_ref[...]
    conv = (cu_ref[SUBLANES - 2:tm + SUBLANES - 2, :] * cw[0:1, :]
            + cu_ref[SUBLANES - 1:tm + SUBLANES - 1, :] * cw[1:2, :]
            + cu * cw[2:3, :])
    y_conv = jnp.dot((bp_ref[...].astype(F32) * conv).astype(BF16), wc_ref[...],
                     preferred_element_type=F32)
    y_attn = jnp.dot(o_ref[...], wa_ref[...], preferred_element_type=F32)
    merged = (jax.nn.sigmoid(gc_ref[...].astype(F32)) * y_conv
              + jax.nn.sigmoid(ga_ref[...].astype(F32)) * y_attn)
    h1 = x_ref[...] + jnp.dot(merged.astype(BF16), wo_ref[...], preferred_element_type=F32)
    h1_ref[...] = h1

    ms = jnp.mean(h1 * h1, axis=-1, keepdims=True)
    n = h1 * lax.rsqrt(ms + RMS_EPS) * gffn_ref[...]
    _store_slabs(n, npk_ref, tm)

    n_hi = n.astype(BF16)
    n_lo = (n - n_hi.astype(F32)).astype(BF16)
    logits = (jnp.dot(n_hi, wrh_ref[...], preferred_element_type=F32)
              + jnp.dot(n_lo, wrh_ref[...], preferred_element_type=F32)
              + jnp.dot(n_hi, wrl_ref[...], preferred_element_type=F32)) + br_ref[...]
    lane = lax.broadcasted_iota(I32, (tm, LANES), 1)
    lg = jnp.where(lane < N_EXPERTS, logits, -jnp.inf)

    sels, tops, idxs = [], [], []
    for _ in range(TOP_K):
        m = jnp.max(lg, axis=-1, keepdims=True)
        idx = jnp.min(jnp.where(lg == m, lane, LANES), axis=-1, keepdims=True)
        sel = lane == idx
        sels.append(sel)
        tops.append(m)
        idxs.append(idx)
        lg = jnp.where(sel, -jnp.inf, lg)
    exps = [jnp.exp(t - tops[0]) for t in tops]
    denom = exps[0] + exps[1] + exps[2] + exps[3]
    wts = [e / denom for e in exps]

    onehot = jnp.zeros((tm, LANES), F32)
    for sel in sels:
        onehot = onehot + sel.astype(F32)
    base = carry_ref[0:1, :] + jnp.dot(ltri_ref[...], onehot.astype(BF16), preferred_element_type=F32)
    meta = jnp.zeros((tm, LANES), I32)
    wlanes = jnp.zeros((tm, LANES), F32)
    for k in range(TOP_K):
        rank = jnp.sum(jnp.where(sels[k], base, 0.0), axis=-1, keepdims=True)
        meta = jnp.where(lane == k, idxs[k], meta)
        meta = jnp.where(lane == TOP_K + k, rank.astype(I32), meta)
        wlanes = jnp.where(lane == k, wts[k], wlanes)
    meta_ref[...] = meta
    wts_ref[...] = wlanes
    carry_ref[0:1, :] = carry_ref[0:1, :] + jnp.sum(onehot, axis=0, keepdims=True)
    cnt_ref[...] = jnp.broadcast_to(carry_ref[0:1, :], cnt_ref.shape).astype(I32)


def _tri_strict_lower(n):
    r = lax.broadcasted_iota(I32, (n, n), 0)
    c = lax.broadcasted_iota(I32, (n, n), 1)
    return (c < r).astype(BF16)


def _const_spec(shape):
    return pl.BlockSpec(shape, lambda i: (0,) * len(shape))


def _mixer(proj, proj_meta, attn_o, x2d, conv_w, wc, wa, wo, g_ffn, wr_hi, wr_lo, b_r, seq, tm):
    n_tok = x2d.shape[0]
    tiles_per_seq = seq // tm
    meta_blk = META_ROWS // (2 * SUBLANES) - 1
    in_specs = [
        pl.BlockSpec((tm, CONV_CH), lambda i: (i, 0)),
        pl.BlockSpec((tm, CONV_CH), lambda i: (i, 1)),
        pl.BlockSpec((tm, CONV_CH), lambda i: (i, 2)),
        pl.BlockSpec((tm, D_MODEL), lambda i: (i, 3)),
        pl.BlockSpec((tm, D_MODEL), lambda i: (i, 4)),
        pl.BlockSpec((tm, ATTN_WIDTH), lambda i: (i, 0)),
        pl.BlockSpec((tm, D_MODEL), lambda i: (i, 0)),
        pl.BlockSpec((2 * SUBLANES, CONV_CH), lambda i: (meta_blk, 0)),
        pl.BlockSpec((2 * SUBLANES, CONV_CH), lambda i: (meta_blk, 2)),
        _const_spec((SUBLANES, CONV_CH)),
        _const_spec((CONV_CH, D_MODEL)),
        _const_spec((ATTN_WIDTH, D_MODEL)),
        _const_spec((D_MODEL, D_MODEL)),
        _const_spec((1, D_MODEL)),
        _const_spec((D_MODEL, LANES)),
        _const_spec((D_MODEL, LANES)),
        _const_spec((1, LANES)),
        _const_spec((tm, tm)),
    ]
    out_specs = [
        pl.BlockSpec((tm, D_MODEL), lambda i: (i, 0)),
        pl.BlockSpec(_tok_shape(tm), lambda i: (0, i, 0)),
        pl.BlockSpec((tm, LANES), lambda i: (i, 0)),
        pl.BlockSpec((tm, LANES), lambda i: (i, 0)),
        _const_spec((SUBLANES, LANES)),
    ]
    out_shape = [
        jax.ShapeDtypeStruct((n_tok, D_MODEL), F32),
        jax.ShapeDtypeStruct(_tok_shape(n_tok), F32),
        jax.ShapeDtypeStruct((n_tok, LANES), I32),
        jax.ShapeDtypeStruct((n_tok, LANES), F32),
        jax.ShapeDtypeStruct((SUBLANES, LANES), I32),
    ]
    conv_w8 = jnp.pad(conv_w, ((0, SUBLANES - CONV_K), (0, 0)))
    return pl.pallas_call(
        functools.partial(_mixer_kernel, tm=tm, tiles_per_seq=tiles_per_seq),
        grid=(n_tok // tm,),
        in_specs=in_specs,
        out_specs=out_specs,
        out_shape=out_shape,
        scratch_shapes=[pltpu.VMEM((tm + SUBLANES, CONV_CH), F32), pltpu.VMEM((SUBLANES, LANES), F32)],
        compiler_params=_cparams(("arbitrary",)),
        name="mixer_out",
    )(proj, proj, proj, proj, proj, attn_o, x2d, proj_meta, proj_meta, conv_w8, wc, wa, wo,
      g_ffn.reshape(1, D_MODEL), wr_hi, wr_lo, b_r, _tri_strict_lower(tm))


def _row_slab(ref, row):
    return ref.at[:, pl.ds(pl.multiple_of(row * TOK_ROWS, TOK_ROWS), TOK_ROWS), :]


def _wait_rows(hbm_ref, vmem_or_hbm_ref, sem, n_rows):
    n_sub = n_rows * TOK_ROWS
    pltpu.make_async_copy(hbm_ref.at[:, pl.ds(0, n_sub), :], vmem_or_hbm_ref.at[:, pl.ds(0, n_sub), :],
                          sem).wait()


def _dispatch_kernel(eid_ref, rank_ref, offs_ref, npk_ref, xg_in_ref, xg_ref, sem, *, tm):
    del xg_in_ref

    def issue(t, c):
        for k in range(TOP_K):
            j = t * TOP_K + k
            dst = offs_ref[eid_ref[j]] + rank_ref[j]
            pltpu.make_async_copy(_row_slab(npk_ref, t), _row_slab(xg_ref, dst), sem).start(priority=k % 2)
        return c

    lax.fori_loop(0, tm, issue, 0, unroll=4)
    _wait_rows(xg_ref, xg_ref, sem, tm * TOP_K)


def _dispatch(eid_flat, rank_flat, offs, npk, xg0, tm):
    n_tok = npk.shape[1] // TOK_ROWS
    n_rows = xg0.shape[1] // TOK_ROWS
    smem_blk = pl.BlockSpec((tm * TOP_K,), lambda i: (i,), memory_space=pltpu.SMEM)
    return pl.pallas_call(
        functools.partial(_dispatch_kernel, tm=tm),
        grid=(n_tok // tm,),
        in_specs=[smem_blk, smem_blk,
                  pl.BlockSpec(memory_space=pltpu.SMEM),
                  pl.BlockSpec(_tok_shape(tm), lambda i: (0, i, 0)),
                  pl.BlockSpec(memory_space=pl.ANY)],
        out_specs=pl.BlockSpec(memory_space=pl.ANY),
        out_shape=jax.ShapeDtypeStruct(_tok_shape(n_rows), F32),
        scratch_shapes=[pltpu.SemaphoreType.DMA(())],
        input_output_aliases={4: 0},
        compiler_params=_cparams(("arbitrary",)),
        name="moe_dispatch",
    )(eid_flat, rank_flat, offs, npk, xg0)


def _expert_kernel(te_ref, tb_ref, tv_ref, nu_ref, x_ref, wg_ref, bg_ref, wu_ref, bu_ref, wd_ref, bd_ref,
                   o_ref, acc_ref, *, tr, sub):
    del te_ref, tb_ref
    t = pl.program_id(0)
    f = pl.program_id(1)

    @pl.when(jnp.logical_and(t == 0, f == 0))
    def _():
        acc_ref[...] = jnp.zeros_like(acc_ref)

    def tile_body(rows):
        x = jnp.concatenate([x_ref[_slab_idx(0, rows, c)].astype(BF16) for c in range(N_CHUNKS)], axis=-1)
        gate = jnp.dot(x, wg_ref[0, 0], preferred_element_type=F32) + bg_ref[0]
        up = jnp.dot(x, wu_ref[0, 0], preferred_element_type=F32) + bu_ref[0]
        gate = jnp.minimum(gate, SWIGLU_LIMIT)
        up = jnp.clip(up, -SWIGLU_LIMIT, SWIGLU_LIMIT)
        act = (up + 1.0) * (gate * jax.nn.sigmoid(SWIGLU_ALPHA * gate))
        prev = jnp.where(f == 0, 0.0, acc_ref[0:rows, :])
        acc = prev + jnp.dot(act.astype(BF16), wd_ref[0], preferred_element_type=F32)
        acc_ref[0:rows, :] = acc
        _store_slabs(acc + bd_ref[0], o_ref, rows)
        if rows < tr:
            o_ref[:, rows * TOK_ROWS:tr * TOK_ROWS, :] = jnp.zeros(
                (TOK_HALVES, (tr - rows) * TOK_ROWS, LANES), F32)

    n_sub = (tv_ref[t] + sub - 1) // sub
    for nb in range(1, tr // sub + 1):
        pl.when(jnp.logical_and(t < nu_ref[0], n_sub == nb))(functools.partial(tile_body, nb * sub))


def _experts(tile_expert, tile_block, tile_valid, n_used, xg, w_gate, b_gate, w_up, b_up, w_down, b_down,
             tr, sub):
    n_rows = xg.shape[1] // TOK_ROWS
    n_exp, n_fc, d, fc = w_gate.shape
    dff = n_fc * fc
    grid_spec = pltpu.PrefetchScalarGridSpec(
        num_scalar_prefetch=4,
        grid=(n_rows // tr, n_fc),
        in_specs=[
            pl.BlockSpec(_tok_shape(tr), lambda t, f, te, tb, tv, nu: (0, tb[t], 0)),
            pl.BlockSpec((1, 1, d, fc), lambda t, f, te, tb, tv, nu: (te[t], f, 0, 0)),
            pl.BlockSpec((1, 1, fc), lambda t, f, te, tb, tv, nu: (te[t], 0, f)),
            pl.BlockSpec((1, 1, d, fc), lambda t, f, te, tb, tv, nu: (te[t], f, 0, 0)),
            pl.BlockSpec((1, 1, fc), lambda t, f, te, tb, tv, nu: (te[t], 0, f)),
            pl.BlockSpec((1, fc, d), lambda t, f, te, tb, tv, nu: (te[t], f, 0)),
            pl.BlockSpec((1, 1, d), lambda t, f, te, tb, tv, nu: (te[t], 0, 0)),
        ],
        out_specs=pl.BlockSpec(_tok_shape(tr), lambda t, f, te, tb, tv, nu: (0, tb[t], 0)),
        scratch_shapes=[pltpu.VMEM((tr, d), F32)],
    )
    return pl.pallas_call(
        functools.partial(_expert_kernel, tr=tr, sub=sub),
        grid_spec=grid_spec,
        out_shape=jax.ShapeDtypeStruct(_tok_shape(n_rows), F32),
        input_output_aliases={4: 0},
        compiler_params=_cparams(("arbitrary", "arbitrary")),
        name="moe_experts",
    )(tile_expert, tile_block, tile_valid, n_used, xg, w_gate, b_gate.reshape(n_exp, 1, dff), w_up,
      b_up.reshape(n_exp, 1, dff), w_down, b_down.reshape(n_exp, 1, d))


def _combine_kernel(eid_ref, rank_ref, eid_next_ref, rank_next_ref, offs_ref, og_ref, wts_ref, h1_ref,
                    gfin_ref, out_ref, gbuf_ref, h2_ref, sems, *, tm):
    i = pl.program_id(0)
    slot = i % 2

    def gather(e_ref, r_ref, dst_slot):
        gdst = gbuf_ref.at[dst_slot]

        def issue(t, c):
            for k in range(TOP_K):
                j = t * TOP_K + k
                src = offs_ref[e_ref[j]] + r_ref[j]
                pltpu.make_async_copy(_row_slab(og_ref, src), _row_slab(gdst, k * tm + t),
                                      sems.at[dst_slot]).start(priority=k % 2)
            return c

        lax.fori_loop(0, tm, issue, 0, unroll=4)

    @pl.when(i == 0)
    def _():
        gather(eid_ref, rank_ref, 0)

    @pl.when(i + 1 < pl.num_programs(0))
    def _():
        gather(eid_next_ref, rank_next_ref, 1 - slot)

    gcur = gbuf_ref.at[slot]
    _wait_rows(og_ref, gcur, sems.at[slot], tm * TOP_K)

    wts = wts_ref[...]
    wk = [wts[:, k:k + 1] for k in range(TOP_K)]
    for c in range(N_CHUNKS):
        y = h1_ref[:, c * LANES:(c + 1) * LANES]
        for k in range(TOP_K):
            y = y + wk[k] * gcur[_slab_idx(k * tm, tm, c)]
        h2_ref[:, c * LANES:(c + 1) * LANES] = y
    h2 = h2_ref[...]
    ms = jnp.mean(h2 * h2, axis=-1, keepdims=True)
    out_ref[...] = h2 * lax.rsqrt(ms + RMS_EPS) * gfin_ref[...]


def _combine(eid_flat, rank_flat, offs, og, wts, h1, g_final, tm):
    n_tok = h1.shape[0]
    n_steps = n_tok // tm
    smem_blk = pl.BlockSpec((tm * TOP_K,), lambda i: (i,), memory_space=pltpu.SMEM)
    smem_next = pl.BlockSpec((tm * TOP_K,), lambda i: (jnp.minimum(i + 1, n_steps - 1),),
                             memory_space=pltpu.SMEM)
    return pl.pallas_call(
        functools.partial(_combine_kernel, tm=tm),
        grid=(n_steps,),
        in_specs=[smem_blk, smem_blk, smem_next, smem_next,
                  pl.BlockSpec(memory_space=pltpu.SMEM),
                  pl.BlockSpec(memory_space=pl.ANY),
                  pl.BlockSpec((tm, LANES), lambda i: (i, 0)),
                  pl.BlockSpec((tm, D_MODEL), lambda i: (i, 0)),
                  pl.BlockSpec((1, D_MODEL), lambda i: (0, 0))],
        out_specs=pl.BlockSpec((tm, D_MODEL), lambda i: (i, 0)),
        out_shape=jax.ShapeDtypeStruct((n_tok, D_MODEL), F32),
        scratch_shapes=[pltpu.VMEM((2,) + _tok_shape(tm * TOP_K), F32),
                        pltpu.VMEM((tm, D_MODEL), F32),
                        pltpu.SemaphoreType.DMA((2,))],
        compiler_params=_cparams(("arbitrary",)),
        name="moe_combine",
    )(eid_flat, rank_flat, eid_flat, rank_flat, offs, og, wts, h1, g_final.reshape(1, D_MODEL))


def _routing_tables(counts, tr, n_tiles):
    ntile = (counts + tr - 1) // tr
    tiles_cum = jnp.cumsum(ntile)
    offs = ((tiles_cum - ntile) * tr).astype(I32)
    n_used = tiles_cum[-1]
    t = jnp.minimum(jnp.arange(n_tiles, dtype=I32), n_used - 1)
    tile_expert = jnp.sum((tiles_cum[None, :] <= t[:, None]).astype(I32), axis=1)
    onehot = (tile_expert[:, None] == jnp.arange(counts.shape[0], dtype=I32)[None, :]).astype(I32)
    expert_end = jnp.sum(onehot * (offs + counts)[None, :], axis=1)
    tile_valid = jnp.clip(expert_end - t * tr, 1, tr).astype(I32)
    return offs, tile_expert, t.astype(I32), tile_valid, n_used.reshape(1).astype(I32)


def kernel(x, meta_tokens, g_mix, w_in, conv_w, w_conv_out, w_attn_out, w_o, g_ffn, w_router,
           b_router, w_gate, b_gate, w_up, b_up, w_down, b_down, g_final):
    assert g_mix.shape[0] == 1, "single-layer trunk"
    bsz, seq, d = x.shape
    n_tok = bsz * seq
    x2d = x.reshape(n_tok, d)

    n_tiles = (n_tok * TOP_K) // EXPERT_TR + N_EXPERTS
    hn = _rmsnorm_bf16(x2d, g_mix[0], RMS_TM)
    proj, xg0 = _in_proj(hn, w_in[0], PROJ_TM, PROJ_TN, zero_rows=n_tiles * EXPERT_TR)
    meta_pad = jnp.pad(meta_tokens.astype(x.dtype), ((META_ROWS - N_META, 0), (0, 0)))
    hn_meta = _rmsnorm_bf16(meta_pad, g_mix[0], META_ROWS)
    proj_meta = _in_proj(hn_meta, w_in[0], META_ROWS, PROJ_TN)

    attn_o, wg_bf, wu_bf, wd_bf = _attention(proj, proj_meta, w_gate[0], w_up[0], w_down[0],
                                             bsz, seq, ATT_TQ, ATT_TK, EXPERT_FC)

    wr = jnp.pad(w_router[0], ((0, 0), (0, LANES - N_EXPERTS)))
    wr_hi = wr.astype(BF16)
    wr_lo = (wr - wr_hi.astype(F32)).astype(BF16)
    b_r = jnp.pad(b_router[0], (0, LANES - N_EXPERTS)).reshape(1, LANES)
    h1, npk, meta, wts, cnt = _mixer(proj, proj_meta, attn_o, x2d, conv_w[0],
                                     w_conv_out[0].astype(BF16), w_attn_out[0].astype(BF16),
                                     w_o[0].astype(BF16), g_ffn[0], wr_hi, wr_lo, b_r, seq, MIX_TM)

    offs, tile_expert, tile_block, tile_valid, n_used = _routing_tables(cnt[0, :N_EXPERTS], EXPERT_TR, n_tiles)
    eid_flat = meta[:, 0:TOP_K].reshape(-1)
    rank_flat = meta[:, TOP_K:2 * TOP_K].reshape(-1)

    xg = _dispatch(eid_flat, rank_flat, offs, npk, xg0, DISPATCH_TM)
    og = _experts(tile_expert, tile_block, tile_valid, n_used, xg, wg_bf, b_gate[0], wu_bf, b_up[0],
                  wd_bf, b_down[0], EXPERT_TR, EXPERT_SUB)
    out = _combine(eid_flat, rank_flat, offs, og, wts, h1, g_final, COMBINE_TM)
    return out.reshape(bsz, seq, d)
```

```python
import functools
import math

import jax
import jax.numpy as jnp
from jax import lax
from jax.experimental import pallas as pl
from jax.experimental.pallas import tpu as pltpu

F32 = jnp.float32
BF16 = jnp.bfloat16
I32 = jnp.int32

D_MODEL = 2048
N_META = 16
N_HEADS = 8
HEAD_DIM = 128
ATTN_WIDTH = N_HEADS * HEAD_DIM
CONV_CH = D_MODEL // 2
CONV_K = 3
N_EXPERTS = 32
TOP_K = 4
D_FF = D_MODEL
SWIGLU_LIMIT = 7.0
SWIGLU_ALPHA = 1.702
RMS_EPS = 1e-5
IN_WIDTH = 3 * CONV_CH + 3 * ATTN_WIDTH + 2 * D_MODEL

LANES = 128
SUBLANES = 8
META_ROWS = 128
LOG2E = 1.4426950408889634
VMEM_LIMIT = 56 * 1024 * 1024
RMS_TM = 512
PROJ_TM = 1024
PROJ_TN = 1024
ATT_TQ = 512
ATT_TK = 256
MIX_TM = 256
DISPATCH_TM = 256
EXPERT_TR = 640
EXPERT_FC = 1024
EXPERT_SUB = 320
COMBINE_TM = 256
TOK_ROWS = SUBLANES
TOK_HALVES = D_MODEL // (TOK_ROWS * LANES)


def _tok_shape(n_rows):
    return (TOK_HALVES, n_rows * TOK_ROWS, LANES)


def _cparams(sem, vmem=VMEM_LIMIT):
    return pltpu.CompilerParams(dimension_semantics=sem, vmem_limit_bytes=vmem)


def _rmsnorm_kernel(x_ref, g_ref, o_ref):
    x = x_ref[...].astype(F32)
    ms = jnp.mean(x * x, axis=-1, keepdims=True)
    o_ref[...] = (x * lax.rsqrt(ms + RMS_EPS) * g_ref[...]).astype(o_ref.dtype)


def _rmsnorm_bf16(x, g, tm):
    m, d = x.shape
    return pl.pallas_call(
        _rmsnorm_kernel,
        grid=(m // tm,),
        in_specs=[pl.BlockSpec((tm, d), lambda i: (i, 0)),
                  pl.BlockSpec((1, d), lambda i: (0, 0))],
        out_specs=pl.BlockSpec((tm, d), lambda i: (i, 0)),
        out_shape=jax.ShapeDtypeStruct((m, d), BF16),
        compiler_params=_cparams(("arbitrary",)),
        name="rmsnorm",
    )(x, g.reshape(1, d))


ZSCALE = LOG2E / math.sqrt(HEAD_DIM)
Q_COL0 = 3 * CONV_CH


def _in_proj_kernel(x_ref, w_ref, o_ref, *rest, q_tile, zero_fill):
    wbf_ref = rest[-1]
    j = pl.program_id(0)

    @pl.when(pl.program_id(1) == 0)
    def _():
        wbf_ref[...] = w_ref[...].astype(BF16)

    acc = jnp.dot(x_ref[...], wbf_ref[...], preferred_element_type=F32)
    o_ref[...] = (acc * jnp.where(j == q_tile, ZSCALE, 1.0)).astype(o_ref.dtype)
    if zero_fill:
        rest[0][...] = jnp.zeros_like(rest[0])


def _in_proj(x, w, tm, tn, zero_rows=0):
    m, k = x.shape
    _, n = w.shape
    assert Q_COL0 % tn == 0 and ATTN_WIDTH == tn
    steps = (n // tn) * (m // tm)
    in_specs = [pl.BlockSpec((tm, k), lambda j, i: (i, 0)),
                pl.BlockSpec((k, tn), lambda j, i: (0, j))]
    out_specs = [pl.BlockSpec((tm, tn), lambda j, i: (i, j))]
    out_shape = [jax.ShapeDtypeStruct((m, n), BF16)]
    if zero_rows:
        nblk = max(d for d in range(1, steps + 1) if zero_rows % d == 0)
        n_i = m // tm
        out_specs.append(pl.BlockSpec(_tok_shape(zero_rows // nblk),
                                      lambda j, i: (0, jnp.minimum(j * n_i + i, nblk - 1), 0)))
        out_shape.append(jax.ShapeDtypeStruct(_tok_shape(zero_rows), F32))
    out = pl.pallas_call(
        functools.partial(_in_proj_kernel, q_tile=Q_COL0 // tn, zero_fill=bool(zero_rows)),
        grid=(n // tn, m // tm),
        in_specs=in_specs,
        out_specs=out_specs,
        out_shape=out_shape,
        scratch_shapes=[pltpu.VMEM((k, tn), BF16)],
        compiler_params=_cparams(("arbitrary", "arbitrary")),
        name="in_proj",
    )(x, w)
    return out if zero_rows else out[0]


def _attn_kernel(q_ref, k_ref, v_ref, km_ref, vm_ref, u_ref, um_ref, wg_ref, wu_ref, wd_ref,
                 o_ref, wgo_ref, wuo_ref, wdo_ref, vt_ref, vmt_ref, acc_ref, r_ref, pvm_ref,
                 *, seq, tq, tk, fc):
    for c in range(D_FF // fc):
        wgo_ref[0, c] = wg_ref[:, c * fc:(c + 1) * fc].astype(BF16)
        wuo_ref[0, c] = wu_ref[:, c * fc:(c + 1) * fc].astype(BF16)
    wdo_ref[...] = wd_ref[...].astype(BF16)

    p = pl.program_id(2)
    n_q = seq // tq
    n_sub = tq // tk

    @pl.when(p == 0)
    def _():
        for j in range(seq // tk):
            vt_ref[j] = v_ref[0, j * tk:(j + 1) * tk, :].astype(F32).T.astype(BF16)
        vmt_ref[...] = vm_ref[...].astype(F32).T.astype(BF16)

    u = u_ref[...]
    krow = lax.broadcasted_iota(I32, (tk, tq), 0)
    qcol = lax.broadcasted_iota(I32, (tk, tq), 1)
    valid_meta = lax.broadcasted_iota(I32, (META_ROWS, tq), 0) >= META_ROWS - N_META

    def q_tile(qi):
        q0 = pl.multiple_of(qi * tq, tq)
        q = q_ref[0, pl.ds(q0, tq), :]
        acc_ref[...] = jnp.zeros_like(acc_ref)
        r_ref[...] = jnp.zeros_like(r_ref)

        def sweep(tiles, oldest=None):
            every = tiles + ([oldest] if oldest is not None else [])
            z2s = [lax.dot_general(k, q, (((1,), (1,)), ((), ())), preferred_element_type=F32)
                   for k, _, _, _ in every]
            cums = []
            for z2, (_, _, ut, mask) in zip(z2s, every):
                e = jnp.exp2(-jnp.abs(z2))
                s2 = jnp.maximum(z2, 0.0) + jnp.log2(1.0 + e)
                if mask is not None:
                    s2 = jnp.where(mask, s2, 0.0)
                cums.append(jnp.dot(ut, s2.astype(BF16), preferred_element_type=F32))
            r = r_ref[...]
            pv = None
            for z2, cum, (_, vt, _, mask) in zip(z2s, cums, tiles):
                w = jnp.exp2(z2 - cum - r)
                if mask is not None:
                    w = jnp.where(mask, w, 0.0)
                part = jnp.dot(vt, w.astype(BF16), preferred_element_type=F32)
                pv = part if pv is None else pv + part
                r = r + cum[0:1, :]
            acc_ref[...] += pv
            r_ref[...] = r
            if oldest is not None:
                _, vt, _, mask = oldest
                w = jnp.where(mask, jnp.exp2(z2s[-1] - cums[-1]), 0.0)
                pvm_ref[...] = jnp.dot(vt, w.astype(BF16), preferred_element_type=F32)

        def key_tile(j, mask):
            k0 = pl.multiple_of(j * tk, tk)
            return (k_ref[0, pl.ds(k0, tk), :], vt_ref[j], u, mask)

        sweep([key_tile(qi * n_sub + d, krow + d * tk < qcol) for d in reversed(range(n_sub))],
              oldest=(km_ref[...], vmt_ref[...], um_ref[...], valid_meta))

        def full(jj, c):
            base = (qi - 2 - 2 * jj) * n_sub
            sweep([key_tile(base + d, None) for d in reversed(range(2 * n_sub))])
            return c

        lax.fori_loop(0, qi // 2, full, 0)

        @pl.when(qi % 2 == 1)
        def _():
            sweep([key_tile(d, None) for d in reversed(range(n_sub))])

        out_t = acc_ref[...] + pvm_ref[...] * jnp.exp2(-r_ref[...])
        o_ref[0, pl.ds(q0, tq), :] = out_t.T.astype(o_ref.dtype)

    q_tile(p)
    q_tile(n_q - 1 - p)


def _tri_upper_incl(n):
    r = lax.broadcasted_iota(I32, (n, n), 0)
    c = lax.broadcasted_iota(I32, (n, n), 1)
    return (c >= r).astype(BF16)


def _attention(proj, proj_meta, w_gate, w_up, w_down, bsz, seq, tq, tk, fc):
    proj3 = proj.reshape(bsz, seq, IN_WIDTH)
    qb = Q_COL0 // HEAD_DIM
    kb, vb = qb + N_HEADS, qb + 2 * N_HEADS
    n_pair = seq // tq // 2
    n_exp, d, dff = w_gate.shape
    n_fc = dff // fc
    steps = bsz * N_HEADS * n_pair
    rc = (n_exp * d) // steps
    assert dff == d and rc * steps == n_exp * d and d % rc == 0 and rc % (2 * SUBLANES) == 0
    blk_per_e = d // rc

    def step(b, h, p):
        return (b * N_HEADS + h) * n_pair + p

    seq_spec = lambda col0: pl.BlockSpec((1, seq, HEAD_DIM), lambda b, h, p: (b, 0, col0 + h))
    w_in_spec = pl.BlockSpec((rc, dff), lambda b, h, p: (step(b, h, p), 0))
    wgu_out_spec = pl.BlockSpec((1, n_fc, rc, fc),
                                lambda b, h, p: (step(b, h, p) // blk_per_e, 0, step(b, h, p) % blk_per_e, 0))
    out, wg_bf, wu_bf, wd_bf = pl.pallas_call(
        functools.partial(_attn_kernel, seq=seq, tq=tq, tk=tk, fc=fc),
        grid=(bsz, N_HEADS, n_pair),
        in_specs=[seq_spec(qb), seq_spec(kb), seq_spec(vb),
                  pl.BlockSpec((META_ROWS, HEAD_DIM), lambda b, h, p: (0, kb + h)),
                  pl.BlockSpec((META_ROWS, HEAD_DIM), lambda b, h, p: (0, vb + h)),
                  pl.BlockSpec((tk, tk), lambda b, h, p: (0, 0)),
                  pl.BlockSpec((META_ROWS, META_ROWS), lambda b, h, p: (0, 0)),
                  w_in_spec, w_in_spec,
                  pl.BlockSpec((rc, d), lambda b, h, p: (step(b, h, p), 0))],
        out_specs=[pl.BlockSpec((1, seq, HEAD_DIM), lambda b, h, p: (b, 0, h)),
                   wgu_out_spec, wgu_out_spec,
                   pl.BlockSpec((rc, d), lambda b, h, p: (step(b, h, p), 0))],
        out_shape=[jax.ShapeDtypeStruct((bsz, seq, ATTN_WIDTH), BF16),
                   jax.ShapeDtypeStruct((n_exp, n_fc, d, fc), BF16),
                   jax.ShapeDtypeStruct((n_exp, n_fc, d, fc), BF16),
                   jax.ShapeDtypeStruct((n_exp * dff, d), BF16)],
        scratch_shapes=[pltpu.VMEM((seq // tk, HEAD_DIM, tk), BF16),
                        pltpu.VMEM((HEAD_DIM, META_ROWS), BF16),
                        pltpu.VMEM((HEAD_DIM, tq), F32),
                        pltpu.VMEM((1, tq), F32),
                        pltpu.VMEM((HEAD_DIM, tq), F32)],
        compiler_params=_cparams(("arbitrary", "arbitrary", "arbitrary")),
        name="stickbreak_attn",
    )(proj3, proj3, proj3, proj_meta, proj_meta, _tri_upper_incl(tk), _tri_upper_incl(META_ROWS),
      w_gate.reshape(n_exp * d, dff), w_up.reshape(n_exp * d, dff), w_down.reshape(n_exp * dff, d))
    return out.reshape(bsz * seq, ATTN_WIDTH), wg_bf, wu_bf, wd_bf.reshape(n_exp, dff, d)


def _slab_idx(first_tok, n_tok, c):
    h, s = divmod(c, TOK_ROWS)
    return (h, pl.ds(first_tok * TOK_ROWS + s, n_tok, stride=TOK_ROWS), slice(None))


N_CHUNKS = D_MODEL // LANES


def _store_slabs(vals, out_ref, n_tok):
    for c in range(N_CHUNKS):
        out_ref[_slab_idx(0, n_tok, c)] = vals[:, c * LANES:(c + 1) * LANES]


def _mixer_kernel(u_ref, bp_ref, cp_ref, gc_ref, ga_ref, o_ref, x_ref, um_ref, cm_ref,
                  convw_ref, wc_ref, wa_ref, wo_ref, gffn_ref, wrh_ref, wrl_ref, br_ref, ltri_ref,
                  h1_ref, npk_ref, meta_ref, wts_ref, cnt_ref,
                  cu_ref, carry_ref, *, tm, tiles_per_seq):
    i = pl.program_id(0)
    first = (i % tiles_per_seq) == 0

    @pl.when(i == 0)
    def _():
        carry_ref[...] = jnp.zeros_like(carry_ref)

    @pl.when(first)
    def _():
        cum = cm_ref[...].astype(F32) * um_ref[...].astype(F32)
        cu_ref[0:SUBLANES, :] = cum[SUBLANES:2 * SUBLANES, :]

    @pl.when(jnp.logical_not(first))
    def _():
        cu_ref[0:SUBLANES, :] = cu_ref[tm:tm + SUBLANES, :]

    cu = cp_ref[...].astype(F32) * u_ref[...].astype(F32)
    cu_ref[SUBLANES:tm + SUBLANES, :] = cu
    cw = convw_ref[...]
    conv = (cu_ref[SUBLANES - 2:tm + SUBLANES - 2, :] * cw[0:1, :]
            + cu_ref[SUBLANES - 1:tm + SUBLANES - 1, :] * cw[1:2, :]
            + cu * cw[2:3, :])
    y_conv = jnp.dot((bp_ref[...].astype(F32) * conv).astype(BF16), wc_ref[...],
                     preferred_element_type=F32)
    y_attn = jnp.dot(o_ref[...], wa_ref[...], preferred_element_type=F32)
    merged = (jax.nn.sigmoid(gc_ref[...].astype(F32)) * y_conv
              + jax.nn.sigmoid(ga_ref[...].astype(F32)) * y_attn)
    h1 = x_ref[...] + jnp.dot(merged.astype(BF16), wo_ref[...], preferred_element_type=F32)
    h1_ref[...] = h1

    ms = jnp.mean(h1 * h1, axis=-1, keepdims=True)
    n = h1 * lax.rsqrt(ms + RMS_EPS) * gffn_ref[...]
    _store_slabs(n, npk_ref, tm)

    n_hi = n.astype(BF16)
    n_lo = (n - n_hi.astype(F32)).astype(BF16)
    logits = (jnp.dot(n_hi, wrh_ref[...], preferred_element_type=F32)
              + jnp.dot(n_lo, wrh_ref[...], preferred_element_type=F32)
              + jnp.dot(n_hi, wrl_ref[...], preferred_element_type=F32)) + br_ref[...]
    lane = lax.broadcasted_iota(I32, (tm, LANES), 1)
    lg = jnp.where(lane < N_EXPERTS, logits, -jnp.inf)

    sels, tops, idxs = [], [], []
    for _ in range(TOP_K):
        m = jnp.max(lg, axis=-1, keepdims=True)
        idx = jnp.min(jnp.where(lg == m, lane, LANES), axis=-1, keepdims=True)
        sel = lane == idx
        sels.append(sel)
        tops.append(m)
        idxs.append(idx)
        lg = jnp.where(sel, -jnp.inf, lg)
    exps = [jnp.exp(t - tops[0]) for t in tops]
    denom = exps[0] + exps[1] + exps[2] + exps[3]
    wts = [e / denom for e in exps]

    onehot = jnp.zeros((tm, LANES), F32)
    for sel in sels:
        onehot = onehot + sel.astype(F32)
    base = carry_ref[0:1, :] + jnp.dot(ltri_ref[...], onehot.astype(BF16), preferred_element_type=F32)
    meta = jnp.zeros((tm, LANES), I32)
    wlanes = jnp.zeros((tm, LANES), F32)
    for k in range(TOP_K):
        rank = jnp.sum(jnp.where(sels[k], base, 0.0), axis=-1, keepdims=True)
        meta = jnp.where(lane == k, idxs[k], meta)
        meta = jnp.where(lane == TOP_K + k, rank.astype(I32), meta)
        wlanes = jnp.where(lane == k, wts[k], wlanes)
    meta_ref[...] = meta
    wts_ref[...] = wlanes
    carry_ref[0:1, :] = carry_ref[0:1, :] + jnp.sum(onehot, axis=0, keepdims=True)
    cnt_ref[...] = jnp.broadcast_to(carry_ref[0:1, :], cnt_ref.shape).astype(I32)


def _tri_strict_lower(n):
    r = lax.broadcasted_iota(I32, (n, n), 0)
    c = lax.broadcasted_iota(I32, (n, n), 1)
    return (c < r).astype(BF16)


def _const_spec(shape):
    return pl.BlockSpec(shape, lambda i: (0,) * len(shape))


def _mixer(proj, proj_meta, attn_o, x2d, conv_w, wc, wa, wo, g_ffn, wr_hi, wr_lo, b_r, seq, tm):
    n_tok = x2d.shape[0]
    tiles_per_seq = seq // tm
    meta_blk = META_ROWS // (2 * SUBLANES) - 1
    in_specs = [
        pl.BlockSpec((tm, CONV_CH), lambda i: (i, 0)),
        pl.BlockSpec((tm, CONV_CH), lambda i: (i, 1)),
        pl.BlockSpec((tm, CONV_CH), lambda i: (i, 2)),
        pl.BlockSpec((tm, D_MODEL), lambda i: (i, 3)),
        pl.BlockSpec((tm, D_MODEL), lambda i: (i, 4)),
        pl.BlockSpec((tm, ATTN_WIDTH), lambda i: (i, 0)),
        pl.BlockSpec((tm, D_MODEL), lambda i: (i, 0)),
        pl.BlockSpec((2 * SUBLANES, CONV_CH), lambda i: (meta_blk, 0)),
        pl.BlockSpec((2 * SUBLANES, CONV_CH), lambda i: (meta_blk, 2)),
        _const_spec((SUBLANES, CONV_CH)),
        _const_spec((CONV_CH, D_MODEL)),
        _const_spec((ATTN_WIDTH, D_MODEL)),
        _const_spec((D_MODEL, D_MODEL)),
        _const_spec((1, D_MODEL)),
        _const_spec((D_MODEL, LANES)),
        _const_spec((D_MODEL, LANES)),
        _const_spec((1, LANES)),
        _const_spec((tm, tm)),
    ]
    out_specs = [
        pl.BlockSpec((tm, D_MODEL), lambda i: (i, 0)),
        pl.BlockSpec(_tok_shape(tm), lambda i: (0, i, 0)),
        pl.BlockSpec((tm, LANES), lambda i: (i, 0)),
        pl.BlockSpec((tm, LANES), lambda i: (i, 0)),
        _const_spec((SUBLANES, LANES)),
    ]
    out_shape = [
        jax.ShapeDtypeStruct((n_tok, D_MODEL), F32),
        jax.ShapeDtypeStruct(_tok_shape(n_tok), F32),
        jax.ShapeDtypeStruct((n_tok, LANES), I32),
        jax.ShapeDtypeStruct((n_tok, LANES), F32),
        jax.ShapeDtypeStruct((SUBLANES, LANES), I32),
    ]
    conv_w8 = jnp.pad(conv_w, ((0, SUBLANES - CONV_K), (0, 0)))
    return pl.pallas_call(
        functools.partial(_mixer_kernel, tm=tm, tiles_per_seq=tiles_per_seq),
        grid=(n_tok // tm,),
        in_specs=in_specs,
        out_specs=out_specs,
        out_shape=out_shape,
        scratch_shapes=[pltpu.VMEM((tm + SUBLANES, CONV_CH), F32), pltpu.VMEM((SUBLANES, LANES), F32)],
        compiler_params=_cparams(("arbitrary",)),
        name="mixer_out",
    )(proj, proj, proj, proj, proj, attn_o, x2d, proj_meta, proj_meta, conv_w8, wc, wa, wo,
      g_ffn.reshape(1, D_MODEL), wr_hi, wr_lo, b_r, _tri_strict_lower(tm))


def _row_slab(ref, row):
    return ref.at[:, pl.ds(pl.multiple_of(row * TOK_ROWS, TOK_ROWS), TOK_ROWS), :]


def _wait_rows(hbm_ref, vmem_or_hbm_ref, sem, n_rows):
    n_sub = n_rows * TOK_ROWS
    pltpu.make_async_copy(hbm_ref.at[:, pl.ds(0, n_sub), :], vmem_or_hbm_ref.at[:, pl.ds(0, n_sub), :],
                          sem).wait()


def _dispatch_kernel(eid_ref, rank_ref, offs_ref, npk_ref, xg_in_ref, xg_ref, sem, *, tm):
    del xg_in_ref

    def issue(t, c):
        for k in range(TOP_K):
            j = t * TOP_K + k
            dst = offs_ref[eid_ref[j]] + rank_ref[j]
            pltpu.make_async_copy(_row_slab(npk_ref, t), _row_slab(xg_ref, dst), sem).start(priority=k % 2)
        return c

    lax.fori_loop(0, tm, issue, 0, unroll=4)
    _wait_rows(xg_ref, xg_ref, sem, tm * TOP_K)


def _dispatch(eid_flat, rank_flat, offs, npk, xg0, tm):
    n_tok = npk.shape[1] // TOK_ROWS
    n_rows = xg0.shape[1] // TOK_ROWS
    smem_blk = pl.BlockSpec((tm * TOP_K,), lambda i: (i,), memory_space=pltpu.SMEM)
    return pl.pallas_call(
        functools.partial(_dispatch_kernel, tm=tm),
        grid=(n_tok // tm,),
        in_specs=[smem_blk, smem_blk,
                  pl.BlockSpec(memory_space=pltpu.SMEM),
                  pl.BlockSpec(_tok_shape(tm), lambda i: (0, i, 0)),
                  pl.BlockSpec(memory_space=pl.ANY)],
        out_specs=pl.BlockSpec(memory_space=pl.ANY),
        out_shape=jax.ShapeDtypeStruct(_tok_shape(n_rows), F32),
        scratch_shapes=[pltpu.SemaphoreType.DMA(())],
        input_output_aliases={4: 0},
        compiler_params=_cparams(("arbitrary",)),
        name="moe_dispatch",
    )(eid_flat, rank_flat, offs, npk, xg0)


def _expert_kernel(te_ref, tb_ref, tv_ref, nu_ref, x_ref, wg_ref, bg_ref, wu_ref, bu_ref, wd_ref, bd_ref,
                   o_ref, acc_ref, *, tr, sub):
    del te_ref, tb_ref
    t = pl.program_id(0)
    f = pl.program_id(1)

    @pl.when(jnp.logical_and(t == 0, f == 0))
    def _():
        acc_ref[...] = jnp.zeros_like(acc_ref)

    def tile_body(rows):
        x = jnp.concatenate([x_ref[_slab_idx(0, rows, c)].astype(BF16) for c in range(N_CHUNKS)], axis=-1)
        gate = jnp.dot(x, wg_ref[0, 0], preferred_element_type=F32) + bg_ref[0]
        up = jnp.dot(x, wu_ref[0, 0], preferred_element_type=F32) + bu_ref[0]
        gate = jnp.minimum(gate, SWIGLU_LIMIT)
        up = jnp.clip(up, -SWIGLU_LIMIT, SWIGLU_LIMIT)
        act = (up + 1.0) * (gate * jax.nn.sigmoid(SWIGLU_ALPHA * gate))
        prev = jnp.where(f == 0, 0.0, acc_ref[0:rows, :])
        acc = prev + jnp.dot(act.astype(BF16), wd_ref[0], preferred_element_type=F32)
        acc_ref[0:rows, :] = acc
        _store_slabs(acc + bd_ref[0], o_ref, rows)
        if rows < tr:
            o_ref[:, rows * TOK_ROWS:tr * TOK_ROWS, :] = jnp.zeros(
                (TOK_HALVES, (tr - rows) * TOK_ROWS, LANES), F32)

    n_sub = (tv_ref[t] + sub - 1) // sub
    for nb in range(1, tr // sub + 1):
        pl.when(jnp.logical_and(t < nu_ref[0], n_sub == nb))(functools.partial(tile_body, nb * sub))


def _experts(tile_expert, tile_block, tile_valid, n_used, xg, w_gate, b_gate, w_up, b_up, w_down, b_down,
             tr, sub):
    n_rows = xg.shape[1] // TOK_ROWS
    n_exp, n_fc, d, fc = w_gate.shape
    dff = n_fc * fc
    grid_spec = pltpu.PrefetchScalarGridSpec(
        num_scalar_prefetch=4,
        grid=(n_rows // tr, n_fc),
        in_specs=[
            pl.BlockSpec(_tok_shape(tr), lambda t, f, te, tb, tv, nu: (0, tb[t], 0)),
            pl.BlockSpec((1, 1, d, fc), lambda t, f, te, tb, tv, nu: (te[t], f, 0, 0)),
            pl.BlockSpec((1, 1, fc), lambda t, f, te, tb, tv, nu: (te[t], 0, f)),
            pl.BlockSpec((1, 1, d, fc), lambda t, f, te, tb, tv, nu: (te[t], f, 0, 0)),
            pl.BlockSpec((1, 1, fc), lambda t, f, te, tb, tv, nu: (te[t], 0, f)),
            pl.BlockSpec((1, fc, d), lambda t, f, te, tb, tv, nu: (te[t], f, 0)),
            pl.BlockSpec((1, 1, d), lambda t, f, te, tb, tv, nu: (te[t], 0, 0)),
        ],
        out_specs=pl.BlockSpec(_tok_shape(tr), lambda t, f, te, tb, tv, nu: (0, tb[t], 0)),
        scratch_shapes=[pltpu.VMEM((tr, d), F32)],
    )
    return pl.pallas_call(
        functools.partial(_expert_kernel, tr=tr, sub=sub),
        grid_spec=grid_spec,
        out_shape=jax.ShapeDtypeStruct(_tok_shape(n_rows), F32),
        input_output_aliases={4: 0},
        compiler_params=_cparams(("arbitrary", "arbitrary")),
        name="moe_experts",
    )(tile_expert, tile_block, tile_valid, n_used, xg, w_gate, b_gate.reshape(n_exp, 1, dff), w_up,
      b_up.reshape(n_exp, 1, dff), w_down, b_down.reshape(n_exp, 1, d))


def _combine_kernel(eid_ref, rank_ref, eid_next_ref, rank_next_ref, offs_ref, og_ref, wts_ref, h1_ref,
                    gfin_ref, out_ref, gbuf_ref, h2_ref, sems, *, tm):
    i = pl.program_id(0)
    slot = i % 2

    def gather(e_ref, r_ref, dst_slot):
        gdst = gbuf_ref.at[dst_slot]

        def issue(t, c):
            for k in range(TOP_K):
                j = t * TOP_K + k
                src = offs_ref[e_ref[j]] + r_ref[j]
                pltpu.make_async_copy(_row_slab(og_ref, src), _row_slab(gdst, k * tm + t),
                                      sems.at[dst_slot]).start(priority=k % 2)
            return c

        lax.fori_loop(0, tm, issue, 0, unroll=4)

    @pl.when(i == 0)
    def _():
        gather(eid_ref, rank_ref, 0)

    @pl.when(i + 1 < pl.num_programs(0))
    def _():
        gather(eid_next_ref, rank_next_ref, 1 - slot)

    gcur = gbuf_ref.at[slot]
    _wait_rows(og_ref, gcur, sems.at[slot], tm * TOP_K)

    wts = wts_ref[...]
    wk = [wts[:, k:k + 1] for k in range(TOP_K)]
    for c in range(N_CHUNKS):
        y = h1_ref[:, c * LANES:(c + 1) * LANES]
        for k in range(TOP_K):
            y = y + wk[k] * gcur[_slab_idx(k * tm, tm, c)]
        h2_ref[:, c * LANES:(c + 1) * LANES] = y
    h2 = h2_ref[...]
    ms = jnp.mean(h2 * h2, axis=-1, keepdims=True)
    out_ref[...] = h2 * lax.rsqrt(ms + RMS_EPS) * gfin_ref[...]


def _combine(eid_flat, rank_flat, offs, og, wts, h1, g_final, tm):
    n_tok = h1.shape[0]
    n_steps = n_tok // tm
    smem_blk = pl.BlockSpec((tm * TOP_K,), lambda i: (i,), memory_space=pltpu.SMEM)
    smem_next = pl.BlockSpec((tm * TOP_K,), lambda i: (jnp.minimum(i + 1, n_steps - 1),),
                             memory_space=pltpu.SMEM)
    return pl.pallas_call(
        functools.partial(_combine_kernel, tm=tm),
        grid=(n_steps,),
        in_specs=[smem_blk, smem_blk, smem_next, smem_next,
                  pl.BlockSpec(memory_space=pltpu.SMEM),
                  pl.BlockSpec(memory_space=pl.ANY),
                  pl.BlockSpec((tm, LANES), lambda i: (i, 0)),
                  pl.BlockSpec((tm, D_MODEL), lambda i: (i, 0)),
                  pl.BlockSpec((1, D_MODEL), lambda i: (0, 0))],
        out_specs=pl.BlockSpec((tm, D_MODEL), lambda i: (i, 0)),
        out_shape=jax.ShapeDtypeStruct((n_tok, D_MODEL), F32),
        scratch_shapes=[pltpu.VMEM((2,) + _tok_shape(tm * TOP_K), F32),
                        pltpu.VMEM((tm, D_MODEL), F32),
                        pltpu.SemaphoreType.DMA((2,))],
        compiler_params=_cparams(("arbitrary",)),
        name="moe_combine",
    )(eid_flat, rank_flat, eid_flat, rank_flat, offs, og, wts, h1, g_final.reshape(1, D_MODEL))


def _routing_tables(counts, tr, n_tiles):
    ntile = (counts + tr - 1) // tr
    tiles_cum = jnp.cumsum(ntile)
    offs = ((tiles_cum - ntile) * tr).astype(I32)
    n_used = tiles_cum[-1]
    t = jnp.minimum(jnp.arange(n_tiles, dtype=I32), n_used - 1)
    tile_expert = jnp.sum((tiles_cum[None, :] <= t[:, None]).astype(I32), axis=1)
    onehot = (tile_expert[:, None] == jnp.arange(counts.shape[0], dtype=I32)[None, :]).astype(I32)
    expert_end = jnp.sum(onehot * (offs + counts)[None, :], axis=1)
    tile_valid = jnp.clip(expert_end - t * tr, 1, tr).astype(I32)
    return offs, tile_expert, t.astype(I32), tile_valid, n_used.reshape(1).astype(I32)


def kernel(x, meta_tokens, g_mix, w_in, conv_w, w_conv_out, w_attn_out, w_o, g_ffn, w_router,
           b_router, w_gate, b_gate, w_up, b_up, w_down, b_down, g_final):
    assert g_mix.shape[0] == 1, "single-layer trunk"
    bsz, seq, d = x.shape
    n_tok = bsz * seq
    x2d = x.reshape(n_tok, d)

    n_tiles = (n_tok * TOP_K) // EXPERT_TR + N_EXPERTS
    hn = _rmsnorm_bf16(x2d, g_mix[0], RMS_TM)
    proj, xg0 = _in_proj(hn, w_in[0], PROJ_TM, PROJ_TN, zero_rows=n_tiles * EXPERT_TR)
    meta_pad = jnp.pad(meta_tokens.astype(x.dtype), ((META_ROWS - N_META, 0), (0, 0)))
    hn_meta = _rmsnorm_bf16(meta_pad, g_mix[0], META_ROWS)
    proj_meta = _in_proj(hn_meta, w_in[0], META_ROWS, PROJ_TN)

    attn_o, wg_bf, wu_bf, wd_bf = _attention(proj, proj_meta, w_gate[0], w_up[0], w_down[0],
                                             bsz, seq, ATT_TQ, ATT_TK, EXPERT_FC)

    wr = jnp.pad(w_router[0], ((0, 0), (0, LANES - N_EXPERTS)))
    wr_hi = wr.astype(BF16)
    wr_lo = (wr - wr_hi.astype(F32)).astype(BF16)
    b_r = jnp.pad(b_router[0], (0, LANES - N_EXPERTS)).reshape(1, LANES)
    h1, npk, meta, wts, cnt = _mixer(proj, proj_meta, attn_o, x2d, conv_w[0],
                                     w_conv_out[0].astype(BF16), w_attn_out[0].astype(BF16),
                                     w_o[0].astype(BF16), g_ffn[0], wr_hi, wr_lo, b_r, seq, MIX_TM)

    offs, tile_expert, tile_block, tile_valid, n_used = _routing_tables(cnt[0, :N_EXPERTS], EXPERT_TR, n_tiles)
    eid_flat = meta[:, 0:TOP_K].reshape(-1)
    rank_flat = meta[:, TOP_K:2 * TOP_K].reshape(-1)

    xg = _dispatch(eid_flat, rank_flat, offs, npk, xg0, DISPATCH_TM)
    og = _experts(tile_expert, tile_block, tile_valid, n_used, xg, wg_bf, b_gate[0], wu_bf, b_up[0],
                  wd_bf, b_down[0], EXPERT_TR, EXPERT_SUB)
    out = _combine(eid_flat, rank_flat, offs, og, wts, h1, g_final, COMBINE_TM)
    return out.reshape(bsz, seq, d)
```

```python
import functools
import math

import jax
import jax.numpy as jnp
from jax import lax
from jax.experimental import pallas as pl
from jax.experimental.pallas import tpu as pltpu

F32 = jnp.float32
BF16 = jnp.bfloat16
I32 = jnp.int32

D_MODEL = 2048
N_META = 16
N_HEADS = 8
HEAD_DIM = 128
ATTN_WIDTH = N_HEADS * HEAD_DIM
CONV_CH = D_MODEL // 2
CONV_K = 3
N_EXPERTS = 32
TOP_K = 4
D_FF = D_MODEL
SWIGLU_LIMIT = 7.0
SWIGLU_ALPHA = 1.702
RMS_EPS = 1e-5
IN_WIDTH = 3 * CONV_CH + 3 * ATTN_WIDTH + 2 * D_MODEL

LANES = 128
SUBLANES = 8
META_ROWS = 128
LOG2E = 1.4426950408889634
VMEM_LIMIT = 56 * 1024 * 1024
RMS_TM = 512
PROJ_TM = 1024
PROJ_TN = 1024
ATT_TQ = 512
ATT_TK = 256
MIX_TM = 256
DISPATCH_TM = 512
EXPERT_TR = 640
EXPERT_FC = 1024
EXPERT_SUB = 320
COMBINE_TM = 256
TOK_ROWS = SUBLANES
TOK_HALVES = D_MODEL // (TOK_ROWS * LANES)


def _tok_shape(n_rows):
    return (TOK_HALVES, n_rows * TOK_ROWS, LANES)


def _cparams(sem, vmem=VMEM_LIMIT):
    return pltpu.CompilerParams(dimension_semantics=sem, vmem_limit_bytes=vmem)


def _rmsnorm_kernel(x_ref, g_ref, o_ref):
    x = x_ref[...].astype(F32)
    ms = jnp.mean(x * x, axis=-1, keepdims=True)
    o_ref[...] = (x * lax.rsqrt(ms + RMS_EPS) * g_ref[...]).astype(o_ref.dtype)


def _rmsnorm_bf16(x, g, tm):
    m, d = x.shape
    return pl.pallas_call(
        _rmsnorm_kernel,
        grid=(m // tm,),
        in_specs=[pl.BlockSpec((tm, d), lambda i: (i, 0)),
                  pl.BlockSpec((1, d), lambda i: (0, 0))],
        out_specs=pl.BlockSpec((tm, d), lambda i: (i, 0)),
        out_shape=jax.ShapeDtypeStruct((m, d), BF16),
        compiler_params=_cparams(("arbitrary",)),
        name="rmsnorm",
    )(x, g.reshape(1, d))


ZSCALE = LOG2E / math.sqrt(HEAD_DIM)
Q_COL0 = 3 * CONV_CH


def _in_proj_kernel(x_ref, w_ref, o_ref, *rest, q_tile, zero_fill):
    wbf_ref = rest[-1]
    j = pl.program_id(0)

    @pl.when(pl.program_id(1) == 0)
    def _():
        wbf_ref[...] = w_ref[...].astype(BF16)

    acc = jnp.dot(x_ref[...], wbf_ref[...], preferred_element_type=F32)
    o_ref[...] = (acc * jnp.where(j == q_tile, ZSCALE, 1.0)).astype(o_ref.dtype)
    if zero_fill:
        rest[0][...] = jnp.zeros_like(rest[0])


def _in_proj(x, w, tm, tn, zero_rows=0):
    m, k = x.shape
    _, n = w.shape
    assert Q_COL0 % tn == 0 and ATTN_WIDTH == tn
    steps = (n // tn) * (m // tm)
    in_specs = [pl.BlockSpec((tm, k), lambda j, i: (i, 0)),
                pl.BlockSpec((k, tn), lambda j, i: (0, j))]
    out_specs = [pl.BlockSpec((tm, tn), lambda j, i: (i, j))]
    out_shape = [jax.ShapeDtypeStruct((m, n), BF16)]
    if zero_rows:
        nblk = max(d for d in range(1, steps + 1) if zero_rows % d == 0)
        n_i = m // tm
        out_specs.append(pl.BlockSpec(_tok_shape(zero_rows // nblk),
                                      lambda j, i: (0, jnp.minimum(j * n_i + i, nblk - 1), 0)))
        out_shape.append(jax.ShapeDtypeStruct(_tok_shape(zero_rows), F32))
    out = pl.pallas_call(
        functools.partial(_in_proj_kernel, q_tile=Q_COL0 // tn, zero_fill=bool(zero_rows)),
        grid=(n // tn, m // tm),
        in_specs=in_specs,
        out_specs=out_specs,
        out_shape=out_shape,
        scratch_shapes=[pltpu.VMEM((k, tn), BF16)],
        compiler_params=_cparams(("arbitrary", "arbitrary")),
        name="in_proj",
    )(x, w)
    return out if zero_rows else out[0]


def _attn_kernel(q_ref, k_ref, v_ref, km_ref, vm_ref, u_ref, um_ref, wg_ref, wu_ref, wd_ref,
                 o_ref, wgo_ref, wuo_ref, wdo_ref, vt_ref, vmt_ref, acc_ref, r_ref, pvm_ref,
                 *, seq, tq, tk, fc):
    for c in range(D_FF // fc):
        wgo_ref[0, c] = wg_ref[:, c * fc:(c + 1) * fc].astype(BF16)
        wuo_ref[0, c] = wu_ref[:, c * fc:(c + 1) * fc].astype(BF16)
    wdo_ref[...] = wd_ref[...].astype(BF16)

    p = pl.program_id(2)
    n_q = seq // tq
    n_sub = tq // tk

    @pl.when(p == 0)
    def _():
        for j in range(seq // tk):
            vt_ref[j] = v_ref[0, j * tk:(j + 1) * tk, :].astype(F32).T.astype(BF16)
        vmt_ref[...] = vm_ref[...].astype(F32).T.astype(BF16)

    u = u_ref[...]
    krow = lax.broadcasted_iota(I32, (tk, tq), 0)
    qcol = lax.broadcasted_iota(I32, (tk, tq), 1)
    valid_meta = lax.broadcasted_iota(I32, (META_ROWS, tq), 0) >= META_ROWS - N_META

    def q_tile(qi):
        q0 = pl.multiple_of(qi * tq, tq)
        q = q_ref[0, pl.ds(q0, tq), :]
        acc_ref[...] = jnp.zeros_like(acc_ref)
        r_ref[...] = jnp.zeros_like(r_ref)

        def sweep(tiles, oldest=None):
            every = tiles + ([oldest] if oldest is not None else [])
            z2s = [lax.dot_general(k, q, (((1,), (1,)), ((), ())), preferred_element_type=F32)
                   for k, _, _, _ in every]
            cums = []
            for z2, (_, _, ut, mask) in zip(z2s, every):
                e = jnp.exp2(-jnp.abs(z2))
                s2 = jnp.maximum(z2, 0.0) + jnp.log2(1.0 + e)
                if mask is not None:
                    s2 = jnp.where(mask, s2, 0.0)
                cums.append(jnp.dot(ut, s2.astype(BF16), preferred_element_type=F32))
            r = r_ref[...]
            pv = None
            for z2, cum, (_, vt, _, mask) in zip(z2s, cums, tiles):
                w = jnp.exp2(z2 - cum - r)
                if mask is not None:
                    w = jnp.where(mask, w, 0.0)
                part = jnp.dot(vt, w.astype(BF16), preferred_element_type=F32)
                pv = part if pv is None else pv + part
                r = r + cum[0:1, :]
            acc_ref[...] += pv
            r_ref[...] = r
            if oldest is not None:
                _, vt, _, mask = oldest
                w = jnp.where(mask, jnp.exp2(z2s[-1] - cums[-1]), 0.0)
                pvm_ref[...] = jnp.dot(vt, w.astype(BF16), preferred_element_type=F32)

        def key_tile(j, mask):
            k0 = pl.multiple_of(j * tk, tk)
            return (k_ref[0, pl.ds(k0, tk), :], vt_ref[j], u, mask)

        sweep([key_tile(qi * n_sub + d, krow + d * tk < qcol) for d in reversed(range(n_sub))],
              oldest=(km_ref[...], vmt_ref[...], um_ref[...], valid_meta))

        def full(jj, c):
            base = (qi - 2 - 2 * jj) * n_sub
            sweep([key_tile(base + d, None) for d in reversed(range(2 * n_sub))])
            return c

        lax.fori_loop(0, qi // 2, full, 0)

        @pl.when(qi % 2 == 1)
        def _():
            sweep([key_tile(d, None) for d in reversed(range(n_sub))])

        out_t = acc_ref[...] + pvm_ref[...] * jnp.exp2(-r_ref[...])
        o_ref[0, pl.ds(q0, tq), :] = out_t.T.astype(o_ref.dtype)

    q_tile(p)
    q_tile(n_q - 1 - p)


def _tri_upper_incl(n):
    r = lax.broadcasted_iota(I32, (n, n), 0)
    c = lax.broadcasted_iota(I32, (n, n), 1)
    return (c >= r).astype(BF16)


def _attention(proj, proj_meta, w_gate, w_up, w_down, bsz, seq, tq, tk, fc):
    proj3 = proj.reshape(bsz, seq, IN_WIDTH)
    qb = Q_COL0 // HEAD_DIM
    kb, vb = qb + N_HEADS, qb + 2 * N_HEADS
    n_pair = seq // tq // 2
    n_exp, d, dff = w_gate.shape
    n_fc = dff // fc
    steps = bsz * N_HEADS * n_pair
    rc = (n_exp * d) // steps
    assert dff == d and rc * steps == n_exp * d and d % rc == 0 and rc % (2 * SUBLANES) == 0
    blk_per_e = d // rc

    def step(b, h, p):
        return (b * N_HEADS + h) * n_pair + p

    seq_spec = lambda col0: pl.BlockSpec((1, seq, HEAD_DIM), lambda b, h, p: (b, 0, col0 + h))
    w_in_spec = pl.BlockSpec((rc, dff), lambda b, h, p: (step(b, h, p), 0))
    wgu_out_spec = pl.BlockSpec((1, n_fc, rc, fc),
                                lambda b, h, p: (step(b, h, p) // blk_per_e, 0, step(b, h, p) % blk_per_e, 0))
    out, wg_bf, wu_bf, wd_bf = pl.pallas_call(
        functools.partial(_attn_kernel, seq=seq, tq=tq, tk=tk, fc=fc),
        grid=(bsz, N_HEADS, n_pair),
        in_specs=[seq_spec(qb), seq_spec(kb), seq_spec(vb),
                  pl.BlockSpec((META_ROWS, HEAD_DIM), lambda b, h, p: (0, kb + h)),
                  pl.BlockSpec((META_ROWS, HEAD_DIM), lambda b, h, p: (0, vb + h)),
                  pl.BlockSpec((tk, tk), lambda b, h, p: (0, 0)),
                  pl.BlockSpec((META_ROWS, META_ROWS), lambda b, h, p: (0, 0)),
                  w_in_spec, w_in_spec,
                  pl.BlockSpec((rc, d), lambda b, h, p: (step(b, h, p), 0))],
        out_specs=[pl.BlockSpec((1, seq, HEAD_DIM), lambda b, h, p: (b, 0, h)),
                   wgu_out_spec, wgu_out_spec,
                   pl.BlockSpec((rc, d), lambda b, h, p: (step(b, h, p), 0))],
        out_shape=[jax.ShapeDtypeStruct((bsz, seq, ATTN_WIDTH), BF16),
                   jax.ShapeDtypeStruct((n_exp, n_fc, d, fc), BF16),
                   jax.ShapeDtypeStruct((n_exp, n_fc, d, fc), BF16),
                   jax.ShapeDtypeStruct((n_exp * dff, d), BF16)],
        scratch_shapes=[pltpu.VMEM((seq // tk, HEAD_DIM, tk), BF16),
                        pltpu.VMEM((HEAD_DIM, META_ROWS), BF16),
                        pltpu.VMEM((HEAD_DIM, tq), F32),
                        pltpu.VMEM((1, tq), F32),
                        pltpu.VMEM((HEAD_DIM, tq), F32)],
        compiler_params=_cparams(("arbitrary", "arbitrary", "arbitrary")),
        name="stickbreak_attn",
    )(proj3, proj3, proj3, proj_meta, proj_meta, _tri_upper_incl(tk), _tri_upper_incl(META_ROWS),
      w_gate.reshape(n_exp * d, dff), w_up.reshape(n_exp * d, dff), w_down.reshape(n_exp * dff, d))
    return out.reshape(bsz * seq, ATTN_WIDTH), wg_bf, wu_bf, wd_bf.reshape(n_exp, dff, d)


def _slab_idx(first_tok, n_tok, c):
    h, s = divmod(c, TOK_ROWS)
    return (h, pl.ds(first_tok * TOK_ROWS + s, n_tok, stride=TOK_ROWS), slice(None))


N_CHUNKS = D_MODEL // LANES


def _store_slabs(vals, out_ref, n_tok):
    for c in range(N_CHUNKS):
        out_ref[_slab_idx(0, n_tok, c)] = vals[:, c * LANES:(c + 1) * LANES]


def _mixer_kernel(u_ref, bp_ref, cp_ref, gc_ref, ga_ref, o_ref, x_ref, um_ref, cm_ref,
                  convw_ref, wc_ref, wa_ref, wo_ref, gffn_ref, wrh_ref, wrl_ref, br_ref, ltri_ref,
                  h1_ref, npk_ref, meta_ref, wts_ref, cnt_ref,
                  cu_ref, carry_ref, *, tm, tiles_per_seq):
    i = pl.program_id(0)
    first = (i % tiles_per_seq) == 0

    @pl.when(i == 0)
    def _():
        carry_ref[...] = jnp.zeros_like(carry_ref)

    @pl.when(first)
    def _():
        cum = cm_ref[...].astype(F32) * um_ref[...].astype(F32)
        cu_ref[0:SUBLANES, :] = cum[SUBLANES:2 * SUBLANES, :]

    @pl.when(jnp.logical_not(first))
    def _():
        cu_ref[0:SUBLANES, :] = cu_ref[tm:tm + SUBLANES, :]

    cu = cp_ref[...].astype(F32) * u_ref[...].astype(F32)
    cu_ref[SUBLANES:tm + SUBLANES, :] = cu
    cw = convw_ref[...]
    conv = (cu_ref[SUBLANES - 2:tm + SUBLANES - 2, :] * cw[0:1, :]
            + cu_ref[SUBLANES - 1:tm + SUBLANES - 1, :] * cw[1:2, :]
            + cu * cw[2:3, :])
    y_conv = jnp.dot((bp_ref[...].astype(F32) * conv).astype(BF16), wc_ref[...],
                     preferred_element_type=F32)
    y_attn = jnp.dot(o_ref[...], wa_ref[...], preferred_element_type=F32)
    merged = (jax.nn.sigmoid(gc_ref[...].astype(F32)) * y_conv
              + jax.nn.sigmoid(ga_ref[...].astype(F32)) * y_attn)
    h1 = x_ref[...] + jnp.dot(merged.astype(BF16), wo_ref[...], preferred_element_type=F32)
    h1_ref[...] = h1

    ms = jnp.mean(h1 * h1, axis=-1, keepdims=True)
    n = h1 * lax.rsqrt(ms + RMS_EPS) * gffn_ref[...]
    _store_slabs(n, npk_ref, tm)

    n_hi = n.astype(BF16)
    n_lo = (n - n_hi.astype(F32)).astype(BF16)
    logits = (jnp.dot(n_hi, wrh_ref[...], preferred_element_type=F32)
              + jnp.dot(n_lo, wrh_ref[...], preferred_element_type=F32)
              + jnp.dot(n_hi, wrl_ref[...], preferred_element_type=F32)) + br_ref[...]
    lane = lax.broadcasted_iota(I32, (tm, LANES), 1)
    lg = jnp.where(lane < N_EXPERTS, logits, -jnp.inf)

    sels, tops, idxs = [], [], []
    for _ in range(TOP_K):
        m = jnp.max(lg, axis=-1, keepdims=True)
        idx = jnp.min(jnp.where(lg == m, lane, LANES), axis=-1, keepdims=True)
        sel = lane == idx
        sels.append(sel)
        tops.append(m)
        idxs.append(idx)
        lg = jnp.where(sel, -jnp.inf, lg)
    exps = [jnp.exp(t - tops[0]) for t in tops]
    denom = exps[0] + exps[1] + exps[2] + exps[3]
    wts = [e / denom for e in exps]

    onehot = jnp.zeros((tm, LANES), F32)
    for sel in sels:
        onehot = onehot + sel.astype(F32)
    base = carry_ref[0:1, :] + jnp.dot(ltri_ref[...], onehot.astype(BF16), preferred_element_type=F32)
    meta = jnp.zeros((tm, LANES), I32)
    wlanes = jnp.zeros((tm, LANES), F32)
    for k in range(TOP_K):
        rank = jnp.sum(jnp.where(sels[k], base, 0.0), axis=-1, keepdims=True)
        meta = jnp.where(lane == k, idxs[k], meta)
        meta = jnp.where(lane == TOP_K + k, rank.astype(I32), meta)
        wlanes = jnp.where(lane == k, wts[k], wlanes)
    meta_ref[...] = meta
    wts_ref[...] = wlanes
    carry_ref[0:1, :] = carry_ref[0:1, :] + jnp.sum(onehot, axis=0, keepdims=True)
    cnt_ref[...] = jnp.broadcast_to(carry_ref[0:1, :], cnt_ref.shape).astype(I32)


def _tri_strict_lower(n):
    r = lax.broadcasted_iota(I32, (n, n), 0)
    c = lax.broadcasted_iota(I32, (n, n), 1)
    return (c < r).astype(BF16)


def _const_spec(shape):
    return pl.BlockSpec(shape, lambda i: (0,) * len(shape))


def _mixer(proj, proj_meta, attn_o, x2d, conv_w, wc, wa, wo, g_ffn, wr_hi, wr_lo, b_r, seq, tm):
    n_tok = x2d.shape[0]
    tiles_per_seq = seq // tm
    meta_blk = META_ROWS // (2 * SUBLANES) - 1
    in_specs = [
        pl.BlockSpec((tm, CONV_CH), lambda i: (i, 0)),
        pl.BlockSpec((tm, CONV_CH), lambda i: (i, 1)),
        pl.BlockSpec((tm, CONV_CH), lambda i: (i, 2)),
        pl.BlockSpec((tm, D_MODEL), lambda i: (i, 3)),
        pl.BlockSpec((tm, D_MODEL), lambda i: (i, 4)),
        pl.BlockSpec((tm, ATTN_WIDTH), lambda i: (i, 0)),
        pl.BlockSpec((tm, D_MODEL), lambda i: (i, 0)),
        pl.BlockSpec((2 * SUBLANES, CONV_CH), lambda i: (meta_blk, 0)),
        pl.BlockSpec((2 * SUBLANES, CONV_CH), lambda i: (meta_blk, 2)),
        _const_spec((SUBLANES, CONV_CH)),
        _const_spec((CONV_CH, D_MODEL)),
        _const_spec((ATTN_WIDTH, D_MODEL)),
        _const_spec((D_MODEL, D_MODEL)),
        _const_spec((1, D_MODEL)),
        _const_spec((D_MODEL, LANES)),
        _const_spec((D_MODEL, LANES)),
        _const_spec((1, LANES)),
        _const_spec((tm, tm)),
    ]
    out_specs = [
        pl.BlockSpec((tm, D_MODEL), lambda i: (i, 0)),
        pl.BlockSpec(_tok_shape(tm), lambda i: (0, i, 0)),
        pl.BlockSpec((tm, LANES), lambda i: (i, 0)),
        pl.BlockSpec((tm, LANES), lambda i: (i, 0)),
        _const_spec((SUBLANES, LANES)),
    ]
    out_shape = [
        jax.ShapeDtypeStruct((n_tok, D_MODEL), F32),
        jax.ShapeDtypeStruct(_tok_shape(n_tok), F32),
        jax.ShapeDtypeStruct((n_tok, LANES), I32),
        jax.ShapeDtypeStruct((n_tok, LANES), F32),
        jax.ShapeDtypeStruct((SUBLANES, LANES), I32),
    ]
    conv_w8 = jnp.pad(conv_w, ((0, SUBLANES - CONV_K), (0, 0)))
    return pl.pallas_call(
        functools.partial(_mixer_kernel, tm=tm, tiles_per_seq=tiles_per_seq),
        grid=(n_tok // tm,),
        in_specs=in_specs,
        out_specs=out_specs,
        out_shape=out_shape,
        scratch_shapes=[pltpu.VMEM((tm + SUBLANES, CONV_CH), F32), pltpu.VMEM((SUBLANES, LANES), F32)],
        compiler_params=_cparams(("arbitrary",)),
        name="mixer_out",
    )(proj, proj, proj, proj, proj, attn_o, x2d, proj_meta, proj_meta, conv_w8, wc, wa, wo,
      g_ffn.reshape(1, D_MODEL), wr_hi, wr_lo, b_r, _tri_strict_lower(tm))


def _row_slab(ref, row):
    return ref.at[:, pl.ds(pl.multiple_of(row * TOK_ROWS, TOK_ROWS), TOK_ROWS), :]


def _wait_rows(hbm_ref, vmem_or_hbm_ref, sem, n_rows):
    n_sub = n_rows * TOK_ROWS
    pltpu.make_async_copy(hbm_ref.at[:, pl.ds(0, n_sub), :], vmem_or_hbm_ref.at[:, pl.ds(0, n_sub), :],
                          sem).wait()


def _dispatch_kernel(eid_ref, rank_ref, offs_ref, npk_ref, xg_in_ref, xg_ref, sem, *, tm):
    del xg_in_ref

    def issue(t, c):
        for k in range(TOP_K):
            j = t * TOP_K + k
            dst = offs_ref[eid_ref[j]] + rank_ref[j]
            pltpu.make_async_copy(_row_slab(npk_ref, t), _row_slab(xg_ref, dst), sem).start(priority=k % 2)
        return c

    lax.fori_loop(0, tm, issue, 0, unroll=4)
    _wait_rows(xg_ref, xg_ref, sem, tm * TOP_K)


def _dispatch(eid_flat, rank_flat, offs, npk, xg0, tm):
    n_tok = npk.shape[1] // TOK_ROWS
    n_rows = xg0.shape[1] // TOK_ROWS
    smem_blk = pl.BlockSpec((tm * TOP_K,), lambda i: (i,), memory_space=pltpu.SMEM)
    return pl.pallas_call(
        functools.partial(_dispatch_kernel, tm=tm),
        grid=(n_tok // tm,),
        in_specs=[smem_blk, smem_blk,
                  pl.BlockSpec(memory_space=pltpu.SMEM),
                  pl.BlockSpec(_tok_shape(tm), lambda i: (0, i, 0)),
                  pl.BlockSpec(memory_space=pl.ANY)],
        out_specs=pl.BlockSpec(memory_space=pl.ANY),
        out_shape=jax.ShapeDtypeStruct(_tok_shape(n_rows), F32),
        scratch_shapes=[pltpu.SemaphoreType.DMA(())],
        input_output_aliases={4: 0},
        compiler_params=_cparams(("arbitrary",)),
        name="moe_dispatch",
    )(eid_flat, rank_flat, offs, npk, xg0)


def _expert_kernel(te_ref, tb_ref, tv_ref, nu_ref, x_ref, wg_ref, bg_ref, wu_ref, bu_ref, wd_ref, bd_ref,
                   o_ref, acc_ref, *, tr, sub):
    del te_ref, tb_ref
    t = pl.program_id(0)
    f = pl.program_id(1)

    @pl.when(jnp.logical_and(t == 0, f == 0))
    def _():
        acc_ref[...] = jnp.zeros_like(acc_ref)

    def tile_body(rows):
        x = jnp.concatenate([x_ref[_slab_idx(0, rows, c)].astype(BF16) for c in range(N_CHUNKS)], axis=-1)
        gate = jnp.dot(x, wg_ref[0, 0], preferred_element_type=F32) + bg_ref[0]
        up = jnp.dot(x, wu_ref[0, 0], preferred_element_type=F32) + bu_ref[0]
        gate = jnp.minimum(gate, SWIGLU_LIMIT)
        up = jnp.clip(up, -SWIGLU_LIMIT, SWIGLU_LIMIT)
        act = (up + 1.0) * (gate * jax.nn.sigmoid(SWIGLU_ALPHA * gate))
        prev = jnp.where(f == 0, 0.0, acc_ref[0:rows, :])
        acc = prev + jnp.dot(act.astype(BF16), wd_ref[0], preferred_element_type=F32)
        acc_ref[0:rows, :] = acc
        _store_slabs(acc + bd_ref[0], o_ref, rows)
        if rows < tr:
            o_ref[:, rows * TOK_ROWS:tr * TOK_ROWS, :] = jnp.zeros(
                (TOK_HALVES, (tr - rows) * TOK_ROWS, LANES), F32)

    n_sub = (tv_ref[t] + sub - 1) // sub
    for nb in range(1, tr // sub + 1):
        pl.when(jnp.logical_and(t < nu_ref[0], n_sub == nb))(functools.partial(tile_body, nb * sub))


def _experts(tile_expert, tile_block, tile_valid, n_used, xg, w_gate, b_gate, w_up, b_up, w_down, b_down,
             tr, sub):
    n_rows = xg.shape[1] // TOK_ROWS
    n_exp, n_fc, d, fc = w_gate.shape
    dff = n_fc * fc
    grid_spec = pltpu.PrefetchScalarGridSpec(
        num_scalar_prefetch=4,
        grid=(n_rows // tr, n_fc),
        in_specs=[
            pl.BlockSpec(_tok_shape(tr), lambda t, f, te, tb, tv, nu: (0, tb[t], 0)),
            pl.BlockSpec((1, 1, d, fc), lambda t, f, te, tb, tv, nu: (te[t], f, 0, 0)),
            pl.BlockSpec((1, 1, fc), lambda t, f, te, tb, tv, nu: (te[t], 0, f)),
            pl.BlockSpec((1, 1, d, fc), lambda t, f, te, tb, tv, nu: (te[t], f, 0, 0)),
            pl.BlockSpec((1, 1, fc), lambda t, f, te, tb, tv, nu: (te[t], 0, f)),
            pl.BlockSpec((1, fc, d), lambda t, f, te, tb, tv, nu: (te[t], f, 0)),
            pl.BlockSpec((1, 1, d), lambda t, f, te, tb, tv, nu: (te[t], 0, 0)),
        ],
        out_specs=pl.BlockSpec(_tok_shape(tr), lambda t, f, te, tb, tv, nu: (0, tb[t], 0)),
        scratch_shapes=[pltpu.VMEM((tr, d), F32)],
    )
    return pl.pallas_call(
        functools.partial(_expert_kernel, tr=tr, sub=sub),
        grid_spec=grid_spec,
        out_shape=jax.ShapeDtypeStruct(_tok_shape(n_rows), F32),
        input_output_aliases={4: 0},
        compiler_params=_cparams(("arbitrary", "arbitrary")),
        name="moe_experts",
    )(tile_expert, tile_block, tile_valid, n_used, xg, w_gate, b_gate.reshape(n_exp, 1, dff), w_up,
      b_up.reshape(n_exp, 1, dff), w_down, b_down.reshape(n_exp, 1, d))


def _combine_kernel(eid_ref, rank_ref, eid_next_ref, rank_next_ref, offs_ref, og_ref, wts_ref, h1_ref,
                    gfin_ref, out_ref, gbuf_ref, h2_ref, sems, *, tm):
    i = pl.program_id(0)
    slot = i % 2

    def gather(e_ref, r_ref, dst_slot):
        gdst = gbuf_ref.at[dst_slot]

        def issue(t, c):
            for k in range(TOP_K):
                j = t * TOP_K + k
                src = offs_ref[e_ref[j]] + r_ref[j]
                pltpu.make_async_copy(_row_slab(og_ref, src), _row_slab(gdst, k * tm + t),
                                      sems.at[dst_slot]).start(priority=k % 2)
            return c

        lax.fori_loop(0, tm, issue, 0, unroll=4)

    @pl.when(i == 0)
    def _():
        gather(eid_ref, rank_ref, 0)

    @pl.when(i + 1 < pl.num_programs(0))
    def _():
        gather(eid_next_ref, rank_next_ref, 1 - slot)

    gcur = gbuf_ref.at[slot]
    _wait_rows(og_ref, gcur, sems.at[slot], tm * TOP_K)

    wts = wts_ref[...]
    wk = [wts[:, k:k + 1] for k in range(TOP_K)]
    for c in range(N_CHUNKS):
        y = h1_ref[:, c * LANES:(c + 1) * LANES]
        for k in range(TOP_K):
            y = y + wk[k] * gcur[_slab_idx(k * tm, tm, c)]
        h2_ref[:, c * LANES:(c + 1) * LANES] = y
    h2 = h2_ref[...]
    ms = jnp.mean(h2 * h2, axis=-1, keepdims=True)
    out_ref[...] = h2 * lax.rsqrt(ms + RMS_EPS) * gfin_ref[...]


def _combine(eid_flat, rank_flat, offs, og, wts, h1, g_final, tm):
    n_tok = h1.shape[0]
    n_steps = n_tok // tm
    smem_blk = pl.BlockSpec((tm * TOP_K,), lambda i: (i,), memory_space=pltpu.SMEM)
    smem_next = pl.BlockSpec((tm * TOP_K,), lambda i: (jnp.minimum(i + 1, n_steps - 1),),
                             memory_space=pltpu.SMEM)
    return pl.pallas_call(
        functools.partial(_combine_kernel, tm=tm),
        grid=(n_steps,),
        in_specs=[smem_blk, smem_blk, smem_next, smem_next,
                  pl.BlockSpec(memory_space=pltpu.SMEM),
                  pl.BlockSpec(memory_space=pl.ANY),
                  pl.BlockSpec((tm, LANES), lambda i: (i, 0)),
                  pl.BlockSpec((tm, D_MODEL), lambda i: (i, 0)),
                  pl.BlockSpec((1, D_MODEL), lambda i: (0, 0))],
        out_specs=pl.BlockSpec((tm, D_MODEL), lambda i: (i, 0)),
        out_shape=jax.ShapeDtypeStruct((n_tok, D_MODEL), F32),
        scratch_shapes=[pltpu.VMEM((2,) + _tok_shape(tm * TOP_K), F32),
                        pltpu.VMEM((tm, D_MODEL), F32),
                        pltpu.SemaphoreType.DMA((2,))],
        compiler_params=_cparams(("arbitrary",)),
        name="moe_combine",
    )(eid_flat, rank_flat, eid_flat, rank_flat, offs, og, wts, h1, g_final.reshape(1, D_MODEL))


def _routing_tables(counts, tr, n_tiles):
    ntile = (counts + tr - 1) // tr
    tiles_cum = jnp.cumsum(ntile)
    offs = ((tiles_cum - ntile) * tr).astype(I32)
    n_used = tiles_cum[-1]
    t = jnp.minimum(jnp.arange(n_tiles, dtype=I32), n_used - 1)
    tile_expert = jnp.sum((tiles_cum[None, :] <= t[:, None]).astype(I32), axis=1)
    onehot = (tile_expert[:, None] == jnp.arange(counts.shape[0], dtype=I32)[None, :]).astype(I32)
    expert_end = jnp.sum(onehot * (offs + counts)[None, :], axis=1)
    tile_valid = jnp.clip(expert_end - t * tr, 1, tr).astype(I32)
    return offs, tile_expert, t.astype(I32), tile_valid, n_used.reshape(1).astype(I32)


def kernel(x, meta_tokens, g_mix, w_in, conv_w, w_conv_out, w_attn_out, w_o, g_ffn, w_router,
           b_router, w_gate, b_gate, w_up, b_up, w_down, b_down, g_final):
    assert g_mix.shape[0] == 1, "single-layer trunk"
    bsz, seq, d = x.shape
    n_tok = bsz * seq
    x2d = x.reshape(n_tok, d)

    n_tiles = (n_tok * TOP_K) // EXPERT_TR + N_EXPERTS
    hn = _rmsnorm_bf16(x2d, g_mix[0], RMS_TM)
    proj, xg0 = _in_proj(hn, w_in[0], PROJ_TM, PROJ_TN, zero_rows=n_tiles * EXPERT_TR)
    meta_pad = jnp.pad(meta_tokens.astype(x.dtype), ((META_ROWS - N_META, 0), (0, 0)))
    hn_meta = _rmsnorm_bf16(meta_pad, g_mix[0], META_ROWS)
    proj_meta = _in_proj(hn_meta, w_in[0], META_ROWS, PROJ_TN)

    attn_o, wg_bf, wu_bf, wd_bf = _attention(proj, proj_meta, w_gate[0], w_up[0], w_down[0],
                                             bsz, seq, ATT_TQ, ATT_TK, EXPERT_FC)

    wr = jnp.pad(w_router[0], ((0, 0), (0, LANES - N_EXPERTS)))
    wr_hi = wr.astype(BF16)
    wr_lo = (wr - wr_hi.astype(F32)).astype(BF16)
    b_r = jnp.pad(b_router[0], (0, LANES - N_EXPERTS)).reshape(1, LANES)
    h1, npk, meta, wts, cnt = _mixer(proj, proj_meta, attn_o, x2d, conv_w[0],
                                     w_conv_out[0].astype(BF16), w_attn_out[0].astype(BF16),
                                     w_o[0].astype(BF16), g_ffn[0], wr_hi, wr_lo, b_r, seq, MIX_TM)

    offs, tile_expert, tile_block, tile_valid, n_used = _routing_tables(cnt[0, :N_EXPERTS], EXPERT_TR, n_tiles)
    eid_flat = meta[:, 0:TOP_K].reshape(-1)
    rank_flat = meta[:, TOP_K:2 * TOP_K].reshape(-1)

    xg = _dispatch(eid_flat, rank_flat, offs, npk, xg0, DISPATCH_TM)
    og = _experts(tile_expert, tile_block, tile_valid, n_used, xg, wg_bf, b_gate[0], wu_bf, b_up[0],
                  wd_bf, b_down[0], EXPERT_TR, EXPERT_SUB)
    out = _combine(eid_flat, rank_flat, offs, og, wts, h1, g_final, COMBINE_TM)
    return out.reshape(bsz, seq, d)
```
